```python
import jax, jax.numpy as jnp
from jax import lax
import numpy as np

D_MODEL = 1024
BATCH = 16
SEQ = 2048
DEPTH = 1
DEC_BATCH = 32
DEC_SEQ = 64
PAST_LEN = 2048

CHUNK = 64
WINDOW = 128
WINDOW_CHUNKS = WINDOW // CHUNK
HEAD_DIM = 64
ATTN_HEADS = 8
KV_HEADS = 2
ATTN_GROUP = ATTN_HEADS // KV_HEADS
RET_HEADS = 8
RET_HEAD_DIM = 64
ATTN_WIDTH = ATTN_HEADS * HEAD_DIM
KV_WIDTH = KV_HEADS * HEAD_DIM
RET_WIDTH = RET_HEADS * RET_HEAD_DIM
MIX_WIDTH = ATTN_WIDTH + RET_WIDTH
IN_WIDTH = ATTN_WIDTH + 2 * KV_WIDTH + 4 * RET_WIDTH
PEER_HEADS = 8
PEER_KEYS = 128
PEER_EXPERTS = PEER_KEYS * PEER_KEYS
PEER_QDIM = 256
PEER_HALF = PEER_QDIM // 2
PEER_TOPK = 16
PEER_BLOCK = 128
EPS = 1e-6
NEG_INF = -1e30

kernel_name = "hymba_swa_retention_peer_stream_step"


def rms_norm(x, g):
    xf = x.astype(jnp.float32)
    y = xf * lax.rsqrt(jnp.mean(xf * xf, axis=-1, keepdims=True) + EPS)
    return (y * g.astype(jnp.float32)).astype(x.dtype)


def alibi_slopes():
    return jnp.asarray(2.0 ** (-8.0 * np.arange(1, ATTN_HEADS + 1) / ATTN_HEADS), dtype=jnp.float32)


def retention_log_decay():
    return jnp.asarray(np.log(1.0 - 2.0 ** (-5.0 - np.arange(RET_HEADS))), dtype=jnp.float32)


def window_attention(q, k_buf, v_buf, pos0, sinks):
    B, T = q.shape[0], q.shape[1]
    L = min(T, CHUNK)
    NB = T // L
    LK = WINDOW + L
    qb = q.reshape(B, NB, L, KV_HEADS, ATTN_GROUP, HEAD_DIM)
    idx = jnp.arange(NB)[:, None] * L + jnp.arange(LK)[None, :]
    kb = k_buf[:, idx]
    vb = v_buf[:, idx]
    q_pos = pos0 + jnp.arange(T).reshape(NB, L)
    k_pos = pos0 - WINDOW + idx
    q_chunk = q_pos // CHUNK
    k_chunk = k_pos // CHUNK
    valid = ((k_pos[:, None, :] >= 0)
             & (k_chunk[:, None, :] <= q_chunk[:, :, None])
             & (k_chunk[:, None, :] >= q_chunk[:, :, None] - WINDOW_CHUNKS))
    dist = jnp.abs(q_pos[:, :, None] - k_pos[:, None, :]).astype(jnp.float32)
    slopes = alibi_slopes().reshape(KV_HEADS, ATTN_GROUP)
    s = jnp.einsum('bnqkgd,bnskd->bnkgqs', qb, kb).astype(jnp.float32) * (HEAD_DIM ** -0.5)
    s = s - slopes[None, None, :, :, None, None] * dist[None, :, None, None, :, :]
    s = jnp.where(valid[None, :, None, None, :, :], s, NEG_INF)
    sink = jnp.broadcast_to(
        sinks.astype(jnp.float32).reshape(KV_HEADS, ATTN_GROUP)[None, None, :, :, None, None],
        s.shape[:-1] + (1,))
    p = jax.nn.softmax(jnp.concatenate([s, sink], axis=-1), axis=-1)[..., :-1]
    o = jnp.einsum('bnkgqs,bnskd->bnqkgd', p.astype(vb.dtype), vb)
    return o.reshape(B, T, ATTN_WIDTH)


def retention(q, k, v, s0):
    f32 = jnp.float32
    B, T = q.shape[0], q.shape[1]
    L = min(T, CHUNK)
    NB = T // L
    qb = q.astype(f32).reshape(B, NB, L, RET_HEADS, RET_HEAD_DIM)
    kb = k.astype(f32).reshape(B, NB, L, RET_HEADS, RET_HEAD_DIM) * (RET_HEAD_DIM ** -0.5)
    vb = v.astype(f32).reshape(B, NB, L, RET_HEADS, RET_HEAD_DIM)
    lg = retention_log_decay()
    i = jnp.arange(L, dtype=f32)
    decay_intra = jnp.exp(jnp.abs(i[:, None] - i[None, :])[None] * lg[:, None, None])
    decay_in = jnp.exp((i + 1.0)[None, :] * lg[:, None])
    decay_out = jnp.exp((L - 1.0 - i)[None, :] * lg[:, None])
    decay_block = jnp.exp(L * lg)[None, :, None, None]
    scores = jnp.einsum('bnihd,bnjhd->bnhij', qb, kb) * decay_intra
    intra = jnp.einsum('bnhij,bnjhe->bnihe', scores, vb)
    kv = jnp.einsum('bnjhd,hj,bnjhe->nbhde', kb, decay_out, vb)

    def step(S, kv_n):
        return decay_block * S + kv_n, S

    s_final, s_prev = lax.scan(step, s0.astype(f32), kv)
    cross = jnp.einsum('bnihd,hi,nbhde->bnihe', qb, decay_in, s_prev)
    return (intra + cross).reshape(B, T, RET_HEADS, RET_HEAD_DIM), s_final


def head_norm(o, g):
    mu = jnp.mean(o, axis=-1, keepdims=True)
    var = jnp.mean(jnp.square(o - mu), axis=-1, keepdims=True)
    y = (o - mu) * lax.rsqrt(var + EPS)
    return y.reshape(o.shape[0], o.shape[1], RET_WIDTH) * g.astype(jnp.float32)


def peer(x, w_q, sub_keys, u_table, v_table):
    shape = x.shape
    xt = x.reshape(-1, D_MODEL)
    n_tok = xt.shape[0]
    pad = (-n_tok) % PEER_BLOCK
    xb = jnp.pad(xt, ((0, pad), (0, 0))).reshape(-1, PEER_BLOCK, D_MODEL)

    def block(xblk):
        q = (xblk @ w_q).reshape(PEER_BLOCK, PEER_HEADS, 2, PEER_HALF)
        s = jnp.einsum('thcd,hcnd->thcn', q, sub_keys).astype(jnp.float32)
        sv, si = lax.top_k(s, PEER_TOPK)
        cand = sv[:, :, 0, :, None] + sv[:, :, 1, None, :]
        cidx = si[:, :, 0, :, None] * PEER_KEYS + si[:, :, 1, None, :]
        cand = cand.reshape(PEER_BLOCK, PEER_HEADS, PEER_TOPK * PEER_TOPK)
        cidx = cidx.reshape(PEER_BLOCK, PEER_HEADS, PEER_TOPK * PEER_TOPK)
        top_s, pos = lax.top_k(cand, PEER_TOPK)
        expert = jnp.take_along_axis(cidx, pos, axis=-1)
        g = jax.nn.softmax(top_s, axis=-1)
        u = u_table[expert]
        hid = jax.nn.gelu(jnp.einsum('thkd,td->thk', u, xblk).astype(jnp.float32), approximate=False)
        v = v_table[expert]
        return jnp.einsum('thk,thkd->td', (g * hid).astype(xblk.dtype), v)

    out = lax.map(block, xb).reshape(-1, D_MODEL)[:n_tok]
    return out.reshape(shape)


def trunk_layer(x, k_cache, v_cache, s0, pos0, attn_norm_g, w_in, attn_sinks, ret_norm_g, w_out,
                ffn_norm_g, peer_w_q, peer_sub_keys, peer_u, peer_v):
    B, T = x.shape[0], x.shape[1]
    h = rms_norm(x, attn_norm_g)
    proj = h @ w_in
    splits = np.cumsum([ATTN_WIDTH, KV_WIDTH, KV_WIDTH, RET_WIDTH, RET_WIDTH, RET_WIDTH]).tolist()
    q_a, k_a, v_a, q_r, k_r, v_r, gate = jnp.split(proj, splits, axis=-1)
    k_buf = jnp.concatenate([k_cache.astype(x.dtype), k_a.reshape(B, T, KV_HEADS, HEAD_DIM)], axis=1)
    v_buf = jnp.concatenate([v_cache.astype(x.dtype), v_a.reshape(B, T, KV_HEADS, HEAD_DIM)], axis=1)
    o_a = window_attention(q_a, k_buf, v_buf, pos0, attn_sinks)
    o_r, s_new = retention(q_r.reshape(B, T, RET_HEADS, RET_HEAD_DIM),
                           k_r.reshape(B, T, RET_HEADS, RET_HEAD_DIM),
                           v_r.reshape(B, T, RET_HEADS, RET_HEAD_DIM), s0)
    o_r = head_norm(o_r, ret_norm_g) * jax.nn.silu(gate.astype(jnp.float32))
    mixed = jnp.concatenate([o_a, o_r.astype(x.dtype)], axis=-1)
    x = x + mixed @ w_out
    x = x + peer(rms_norm(x, ffn_norm_g), peer_w_q, peer_sub_keys, peer_u, peer_v)
    return x, k_buf[:, -WINDOW:], v_buf[:, -WINDOW:], s_new.astype(s0.dtype)


def setup_inputs(seed: int = 0) -> dict:
    key = jax.random.key(seed)
    ks = jax.random.split(key, 18)
    f32 = jnp.float32
    nrm = lambda k, shape, scale: jax.random.normal(k, shape, f32) * scale
    return {
        "x_prompt": nrm(ks[0], (BATCH, SEQ, D_MODEL), 1.0),
        "x_sample": nrm(ks[1], (DEC_BATCH, DEC_SEQ, D_MODEL), 1.0),
        "cache_attn_k": nrm(ks[2], (DEPTH, DEC_BATCH, WINDOW, KV_HEADS, HEAD_DIM), 1.0),
        "cache_attn_v": nrm(ks[3], (DEPTH, DEC_BATCH, WINDOW, KV_HEADS, HEAD_DIM), 1.0),
        "state_ret": nrm(ks[4], (DEPTH, DEC_BATCH, RET_HEADS, RET_HEAD_DIM, RET_HEAD_DIM), 1.0),
        "attn_norm_g": 1.0 + nrm(ks[5], (DEPTH, D_MODEL), 0.01),
        "w_in": nrm(ks[6], (DEPTH, D_MODEL, IN_WIDTH), D_MODEL ** -0.5),
        "attn_sinks": nrm(ks[7], (DEPTH, ATTN_HEADS), 0.5),
        "ret_norm_g": 1.0 + nrm(ks[8], (DEPTH, RET_WIDTH), 0.01),
        "w_out": nrm(ks[9], (DEPTH, MIX_WIDTH, D_MODEL), MIX_WIDTH ** -0.5),
        "ffn_norm_g": 1.0 + nrm(ks[10], (DEPTH, D_MODEL), 0.01),
        "peer_w_q": nrm(ks[11], (DEPTH, D_MODEL, PEER_HEADS * PEER_QDIM), D_MODEL ** -0.5),
        "peer_sub_keys": nrm(ks[12], (DEPTH, PEER_HEADS, 2, PEER_KEYS, PEER_HALF), PEER_HALF ** -0.5),
        "peer_u": nrm(ks[13], (DEPTH, PEER_EXPERTS, D_MODEL), D_MODEL ** -0.5),
        "peer_v": nrm(ks[14], (DEPTH, PEER_EXPERTS, D_MODEL), 0.5),
        "final_norm_g": 1.0 + nrm(ks[15], (D_MODEL,), 0.01),
    }


def reference(x_prompt, x_sample, cache_attn_k, cache_attn_v, state_ret, attn_norm_g, w_in, attn_sinks,
              ret_norm_g, w_out, ffn_norm_g, peer_w_q, peer_sub_keys, peer_u, peer_v, final_norm_g):
    yp, ys = x_prompt, x_sample
    bp = x_prompt.shape[0]
    kps, vps, sps, kss, vss, sss = [], [], [], [], [], []
    for l in range(DEPTH):
        params = (attn_norm_g[l], w_in[l], attn_sinks[l], ret_norm_g[l], w_out[l], ffn_norm_g[l],
                  peer_w_q[l], peer_sub_keys[l], peer_u[l], peer_v[l])
        zero_k = jnp.zeros((bp, WINDOW, KV_HEADS, HEAD_DIM), x_prompt.dtype)
        zero_s = jnp.zeros((bp, RET_HEADS, RET_HEAD_DIM, RET_HEAD_DIM), x_prompt.dtype)
        yp, kp, vp, sp = trunk_layer(yp, zero_k, zero_k, zero_s, 0, *params)
        ys, k_s, v_s, s_s = trunk_layer(ys, cache_attn_k[l], cache_attn_v[l], state_ret[l], PAST_LEN, *params)
        kps.append(kp); vps.append(vp); sps.append(sp)
        kss.append(k_s); vss.append(v_s); sss.append(s_s)
    y_prompt = rms_norm(yp, final_norm_g)
    y_sample = rms_norm(ys, final_norm_g)
    return (y_prompt, y_sample, jnp.stack(kps), jnp.stack(vps), jnp.stack(sps),
            jnp.stack(kss), jnp.stack(vss), jnp.stack(sss))
```

```python
import functools
import math

import numpy as np
import jax
import jax.numpy as jnp
from jax import lax
from jax.experimental import pallas as pl
from jax.experimental.pallas import tpu as pltpu

D_MODEL = 1024
CHUNK = 64
WINDOW = 128
HEAD_DIM = 64
ATTN_HEADS = 8
KV_HEADS = 2
ATTN_GROUP = ATTN_HEADS // KV_HEADS
RET_HEADS = 8
ATTN_WIDTH = ATTN_HEADS * HEAD_DIM
KV_WIDTH = KV_HEADS * HEAD_DIM
RET_WIDTH = RET_HEADS * HEAD_DIM
IN_SPLITS = (ATTN_WIDTH, KV_WIDTH, KV_WIDTH, RET_WIDTH, RET_WIDTH, RET_WIDTH, RET_WIDTH)
IN_WIDTH = sum(IN_SPLITS)
PEER_HEADS = 8
PEER_KEYS = 128
PEER_EXPERTS = PEER_KEYS * PEER_KEYS
PEER_TOPK = 16
PEER_PAIRS = PEER_HEADS * PEER_TOPK
EPS = 1e-6
NEG_INF = -1e30
PAST_LEN = 2048

LANES = 128
SUBLANES = 8
ROW_TILES = D_MODEL // LANES
HALF_TILES = ROW_TILES // 2
VMEM_LIMIT = 48 * 1024 * 1024

TOKEN_TILE = 256
PEER_TILE = 128

BF16 = jnp.bfloat16
F32 = jnp.float32
HI_MASK = -65536

ALIBI_SLOPES = tuple(float(2.0 ** (-8.0 * h / ATTN_HEADS)) for h in range(1, ATTN_HEADS + 1))
RET_LOG_DECAY = tuple(float(np.log(1.0 - 2.0 ** (-5.0 - h))) for h in range(RET_HEADS))


def _rms(x, g):
    return x * lax.rsqrt(jnp.mean(x * x, axis=-1, keepdims=True) + EPS) * g


def _params(*sem):
    return pltpu.CompilerParams(dimension_semantics=sem, vmem_limit_bytes=VMEM_LIMIT)


def _inproj_kernel(x_ref, g_ref, w_ref, *out_refs):
    h = _rms(x_ref[...], g_ref[...]).astype(BF16)
    proj = jnp.dot(h, w_ref[...], preferred_element_type=F32)
    off = 0
    for o_ref, width in zip(out_refs, IN_SPLITS):
        o_ref[...] = proj[:, off:off + width]
        off += width


def _inproj(x, g, w_bf16):
    n = x.shape[0]
    tm = TOKEN_TILE
    return pl.pallas_call(
        _inproj_kernel,
        grid=(n // tm,),
        in_specs=[
            pl.BlockSpec((tm, D_MODEL), lambda i: (i, 0)),
            pl.BlockSpec((1, D_MODEL), lambda i: (0, 0)),
            pl.BlockSpec((D_MODEL, IN_WIDTH), lambda i: (0, 0)),
        ],
        out_specs=[pl.BlockSpec((tm, w), lambda i: (i, 0)) for w in IN_SPLITS],
        out_shape=[jax.ShapeDtypeStruct((n, w), F32) for w in IN_SPLITS],
        compiler_params=_params("arbitrary"),
        name="inproj",
    )(x, g, w_bf16)


def _mixer_kernel(pos_off, nb, qa_ref, k0_ref, k1_ref, k2_ref, v0_ref, v1_ref, v2_ref, qr_ref, kr_ref, vr_ref,
                  gate_ref, s0_ref, sink_ref, rg_ref, dintra_ref, din_ref, dout_ref, mixed_ref, sfin_ref, s_ref):
    n = pl.program_id(1)

    @pl.when(n == 0)
    def _():
        s_ref[...] = s0_ref[0]

    kk = jnp.concatenate([k0_ref[0], k1_ref[0], k2_ref[0]], axis=0)
    vv = jnp.concatenate([v0_ref[0], v1_ref[0], v2_ref[0]], axis=0)
    qa = qa_ref[0]
    lk = WINDOW + CHUNK
    qi = lax.broadcasted_iota(jnp.int32, (CHUNK, lk), 0)
    kj = lax.broadcasted_iota(jnp.int32, (CHUNK, lk), 1)
    dist = jnp.abs(qi + WINDOW - kj).astype(F32)
    valid = (kj + (pos_off + n * CHUNK)) >= 0
    outs = []
    for kh in range(KV_HEADS):
        kmat = kk[:, kh * HEAD_DIM:(kh + 1) * HEAD_DIM].astype(BF16)
        vmat = vv[:, kh * HEAD_DIM:(kh + 1) * HEAD_DIM].astype(BF16)
        for g in range(ATTN_GROUP):
            h = kh * ATTN_GROUP + g
            q = qa[:, h * HEAD_DIM:(h + 1) * HEAD_DIM].astype(BF16)
            s = lax.dot_general(q, kmat, (((1,), (1,)), ((), ())), preferred_element_type=F32) * (HEAD_DIM ** -0.5)
            s = jnp.where(valid, s - ALIBI_SLOPES[h] * dist, NEG_INF)
            sink = sink_ref[0, h]
            m = jnp.maximum(jnp.max(s, axis=-1, keepdims=True), sink)
            p = jnp.exp(s - m)
            denom = jnp.sum(p, axis=-1, keepdims=True) + jnp.exp(sink - m)
            p = p / denom
            outs.append(jnp.dot(p.astype(BF16), vmat, preferred_element_type=F32))

    qr = qr_ref[0]
    kr = kr_ref[0]
    vr = vr_ref[0]
    gate = gate_ref[0]
    rg = rg_ref[...]
    for h in range(RET_HEADS):
        sl = slice(h * HEAD_DIM, (h + 1) * HEAD_DIM)
        q = qr[:, sl]
        k = kr[:, sl] * (HEAD_DIM ** -0.5)
        v = vr[:, sl].astype(BF16)
        s_prev = s_ref[h]
        scores = lax.dot_general(q.astype(BF16), k.astype(BF16), (((1,), (1,)), ((), ())), preferred_element_type=F32)
        scores = scores * dintra_ref[h]
        intra = jnp.dot(scores.astype(BF16), v, preferred_element_type=F32)
        kd = (k * dout_ref[h]).astype(BF16)
        kv = lax.dot_general(kd, v, (((0,), (0,)), ((), ())), preferred_element_type=F32)
        qd = (q * din_ref[h]).astype(BF16)
        cross = jnp.dot(qd, s_prev.astype(BF16), preferred_element_type=F32)
        s_ref[h] = math.exp(CHUNK * RET_LOG_DECAY[h]) * s_prev + kv
        o = intra + cross
        mu = jnp.mean(o, axis=-1, keepdims=True)
        oc = o - mu
        var = jnp.mean(oc * oc, axis=-1, keepdims=True)
        y = oc * lax.rsqrt(var + EPS) * rg[:, sl]
        gt = gate[:, sl]
        outs.append(y * (gt * jax.nn.sigmoid(gt)))
    mixed_ref[0] = jnp.concatenate(outs, axis=-1)

    @pl.when(n == nb - 1)
    def _():
        sfin_ref[0] = s_ref[...]


def _mixer(qa, kbuf, vbuf, qr, kr, vr, gate, s0, sinks, rg, dintra, din, dout, pos0):
    b, t, _ = qa.shape
    nb = t // CHUNK
    tok = lambda w: pl.BlockSpec((1, CHUNK, w), lambda i, j: (i, j, 0))
    kvs = [pl.BlockSpec((1, CHUNK, KV_WIDTH), functools.partial(lambda i, j, o: (i, j + o, 0), o=o)) for o in range(3)]
    state = pl.BlockSpec((1, RET_HEADS, HEAD_DIM, HEAD_DIM), lambda i, j: (i, 0, 0, 0))
    const3 = pl.BlockSpec((RET_HEADS, CHUNK, HEAD_DIM), lambda i, j: (0, 0, 0))
    return pl.pallas_call(
        functools.partial(_mixer_kernel, pos0 - WINDOW, nb),
        grid=(b, nb),
        in_specs=[tok(ATTN_WIDTH)] + kvs + kvs + [tok(RET_WIDTH)] * 4 + [
            state,
            pl.BlockSpec(memory_space=pltpu.SMEM),
            pl.BlockSpec((1, RET_WIDTH), lambda i, j: (0, 0)),
            const3, const3, const3,
        ],
        out_specs=[pl.BlockSpec((1, CHUNK, D_MODEL), lambda i, j: (i, j, 0)), state],
        out_shape=[
            jax.ShapeDtypeStruct((b, t, D_MODEL), F32),
            jax.ShapeDtypeStruct((b, RET_HEADS, HEAD_DIM, HEAD_DIM), F32),
        ],
        scratch_shapes=[pltpu.VMEM((RET_HEADS, HEAD_DIM, HEAD_DIM), F32)],
        compiler_params=_params("arbitrary", "arbitrary"),
        name="mixer",
    )(qa, kbuf, kbuf, kbuf, vbuf, vbuf, vbuf, qr, kr, vr, gate, s0, sinks, rg, dintra, din, dout)


def _decay_tables():
    lg = np.asarray(RET_LOG_DECAY, np.float32)
    i = np.arange(CHUNK, dtype=np.float32)
    dintra = np.exp(np.abs(i[:, None] - i[None, :])[None] * lg[:, None, None])
    din = np.broadcast_to(np.exp((i + 1.0)[None, :] * lg[:, None])[:, :, None], dintra.shape)
    dout = np.broadcast_to(np.exp((CHUNK - 1.0 - i)[None, :] * lg[:, None])[:, :, None], dintra.shape)
    return tuple(jnp.asarray(a, F32) for a in (dintra, din, dout))


def _topk_rows(s, k, payload=None):
    rows = s.shape[0]
    iota = lax.broadcasted_iota(jnp.int32, s.shape, 0)
    vals, idxs = [], []
    for _ in range(k):
        m = jnp.max(s, axis=0, keepdims=True)
        pos = jnp.min(jnp.where(s == m, iota, rows), axis=0, keepdims=True)
        hit = iota == pos
        vals.append(m)
        if payload is None:
            idxs.append(pos)
        else:
            idxs.append(jnp.max(jnp.where(hit, payload, -1), axis=0, keepdims=True))
        s = jnp.where(hit, -jnp.inf, s)
    return jnp.concatenate(vals, axis=0), jnp.concatenate(idxs, axis=0)


def _route_kernel(mixed_ref, x_ref, wout_ref, fg_ref, wqt_ref, subk_ref, x1_ref, xn_ref, idx_ref, gw_ref):
    x1 = x_ref[...] + jnp.dot(mixed_ref[...].astype(BF16), wout_ref[...], preferred_element_type=F32)
    x1_ref[...] = x1
    xn = _rms(x1, fg_ref[...])
    xn_ref[...] = xn
    qt = lax.dot_general(wqt_ref[...], xn.astype(BF16), (((1,), (1,)), ((), ())), preferred_element_type=F32)
    for h in range(PEER_HEADS):
        sv, si = [], []
        for c in range(2):
            hc = h * 2 + c
            qh = qt[hc * PEER_KEYS:(hc + 1) * PEER_KEYS, :].astype(BF16)
            s = jnp.dot(subk_ref[hc], qh, preferred_element_type=F32)
            v, i = _topk_rows(s, PEER_TOPK)
            sv.append(v)
            si.append(i)
        cand = jnp.concatenate([sv[0][a:a + 1, :] + sv[1] for a in range(PEER_TOPK)], axis=0)
        cidx = jnp.concatenate([si[0][a:a + 1, :] * PEER_KEYS + si[1] for a in range(PEER_TOPK)], axis=0)
        top_s, expert = _topk_rows(cand, PEER_TOPK, payload=cidx)
        e = jnp.exp(top_s - top_s[0:1, :])
        gw_ref[h * PEER_TOPK:(h + 1) * PEER_TOPK, :] = e / jnp.sum(e, axis=0, keepdims=True)
        idx_ref[h * PEER_TOPK:(h + 1) * PEER_TOPK, :] = expert


def _route(mixed, x, wout_bf16, fg, wqt_bf16, subk_bf16):
    n = x.shape[0]
    tm = TOKEN_TILE
    tok = pl.BlockSpec((tm, D_MODEL), lambda i: (i, 0))
    pair = pl.BlockSpec((PEER_PAIRS, tm), lambda i: (0, i))
    return pl.pallas_call(
        _route_kernel,
        grid=(n // tm,),
        in_specs=[
            tok, tok,
            pl.BlockSpec((D_MODEL, D_MODEL), lambda i: (0, 0)),
            pl.BlockSpec((1, D_MODEL), lambda i: (0, 0)),
            pl.BlockSpec((2 * PEER_HEADS * PEER_KEYS, D_MODEL), lambda i: (0, 0)),
            pl.BlockSpec((2 * PEER_HEADS, PEER_KEYS, PEER_KEYS), lambda i: (0, 0, 0)),
        ],
        out_specs=[tok, tok, pair, pair],
        out_shape=[
            jax.ShapeDtypeStruct((n, D_MODEL), F32),
            jax.ShapeDtypeStruct((n, D_MODEL), F32),
            jax.ShapeDtypeStruct((PEER_PAIRS, n), jnp.int32),
            jax.ShapeDtypeStruct((PEER_PAIRS, n), F32),
        ],
        compiler_params=_params("arbitrary"),
        name="route",
    )(mixed, x, wout_bf16, fg, wqt_bf16, subk_bf16)


def _pack_table(t):
    half = D_MODEL // 2
    b = lax.bitcast_convert_type(t.astype(BF16), jnp.uint16).astype(jnp.uint32)
    word = b[:, :half] | (b[:, half:] << 16)
    return lax.bitcast_convert_type(word, jnp.int32).reshape(t.shape[0], HALF_TILES, LANES)


def _unpack(wd):
    lo = pltpu.bitcast(jnp.left_shift(wd, 16), F32)
    hi = pltpu.bitcast(jnp.bitwise_and(wd, jnp.int32(HI_MASK)), F32)
    return lo, hi


def _reduce8(prods, sub):
    a = [prods[i] for i in (0, 4, 2, 6, 1, 5, 3, 7)]
    m = [jnp.concatenate([a[2 * i], a[2 * i + 1]], axis=0) for i in range(4)]
    m2 = (sub & 2) == 0
    m1 = (sub & 1) == 0

    def lvl2(x, y):
        return jnp.where(m2, x + pltpu.roll(x, 6, 0), y + pltpu.roll(y, 2, 0))

    def lvl1(x, y):
        return jnp.where(m1, x + pltpu.roll(x, 7, 0), y + pltpu.roll(y, 1, 0))

    return lvl1(lvl2(m[0], m[1]), lvl2(m[2], m[3]))


def _peer_u_kernel(idx_ref, xn_ref, g_ref, tab_ref, w_ref, hacc_ref):
    sub = lax.broadcasted_iota(jnp.int32, (SUBLANES, LANES), 0)
    lane = lax.broadcasted_iota(jnp.int32, (PEER_PAIRS, PEER_TILE), 1)

    def tok(t, carry):
        x = xn_ref[t]
        x_lo = x[0:HALF_TILES]
        x_hi = x[HALF_TILES:ROW_TILES]
        cols = []
        for grp in range(PEER_PAIRS // 8):
            prods = []
            for j in range(8):
                lo, hi = _unpack(tab_ref[idx_ref[t, grp * 8 + j]])
                prods.append(lo * x_lo + hi * x_hi)
            cols.append(jnp.sum(_reduce8(prods, sub), axis=-1, keepdims=True))
        col = jnp.concatenate(cols, axis=0)
        hacc_ref[...] = jnp.where(lane == t, col, hacc_ref[...])
        return carry

    lax.fori_loop(0, PEER_TILE, tok, 0)
    hid = hacc_ref[...].T
    w_ref[...] = g_ref[...] * (0.5 * hid * (1.0 + lax.erf(hid * math.sqrt(0.5))))


def _table_spec():
    return pl.BlockSpec((PEER_EXPERTS, HALF_TILES, LANES), lambda i: (0, 0, 0), pipeline_mode=pl.Buffered(1))


def _peer_u(idx, xn3, g, tab):
    n = idx.shape[0]
    smem = pl.BlockSpec((PEER_TILE, PEER_PAIRS), lambda i: (i, 0), memory_space=pltpu.SMEM)
    return pl.pallas_call(
        _peer_u_kernel,
        grid=(n // PEER_TILE,),
        in_specs=[
            smem,
            pl.BlockSpec((PEER_TILE, ROW_TILES, LANES), lambda i: (i, 0, 0)),
            pl.BlockSpec((PEER_TILE, PEER_PAIRS), lambda i: (i, 0)),
            _table_spec(),
        ],
        out_specs=pl.BlockSpec((PEER_TILE, PEER_PAIRS), lambda i: (i, 0)),
        out_shape=jax.ShapeDtypeStruct((n, PEER_PAIRS), F32),
        scratch_shapes=[pltpu.VMEM((PEER_PAIRS, PEER_TILE), F32)],
        compiler_params=_params("arbitrary"),
        name="peer_u",
    )(idx, xn3, g, tab)


def _peer_v_kernel(idx_ref, w_ref, x1_ref, fg_ref, tab_ref, y_ref):
    fg = fg_ref[...]

    def tok(t, carry):
        accs = [jnp.zeros((HALF_TILES, LANES), F32) for _ in range(8)]
        for p in range(PEER_PAIRS):
            wv = w_ref[t, p]
            lo, hi = _unpack(tab_ref[idx_ref[t, p]])
            k = (p % 4) * 2
            accs[k] = accs[k] + wv * lo
            accs[k + 1] = accs[k + 1] + wv * hi
        a_lo = (accs[0] + accs[2]) + (accs[4] + accs[6])
        a_hi = (accs[1] + accs[3]) + (accs[5] + accs[7])
        x2 = x1_ref[t] + jnp.concatenate([a_lo, a_hi], axis=0)
        ms = jnp.sum(jnp.sum(x2 * x2, axis=1, keepdims=True), axis=0, keepdims=True) * (1.0 / D_MODEL)
        y_ref[t] = x2 * lax.rsqrt(ms + EPS) * fg
        return carry

    lax.fori_loop(0, PEER_TILE, tok, 0)


def _peer_v(idx, w, x13, fg, tab):
    n = idx.shape[0]
    smem = pl.BlockSpec((PEER_TILE, PEER_PAIRS), lambda i: (i, 0), memory_space=pltpu.SMEM)
    row = pl.BlockSpec((PEER_TILE, ROW_TILES, LANES), lambda i: (i, 0, 0))
    return pl.pallas_call(
        _peer_v_kernel,
        grid=(n // PEER_TILE,),
        in_specs=[smem, smem, row, pl.BlockSpec((ROW_TILES, LANES), lambda i: (0, 0)), _table_spec()],
        out_specs=row,
        out_shape=jax.ShapeDtypeStruct((n, ROW_TILES, LANES), F32),
        compiler_params=_params("arbitrary"),
        name="peer_v",
    )(idx, w, x13, fg, tab)


def _trunk(x3, k_cache, v_cache, s0, pos0, p):
    b, t, _ = x3.shape
    n = b * t
    x = x3.reshape(n, D_MODEL)
    qa, ka, va, qr, kr, vr, gate = _inproj(x, p["attn_g"], p["w_in"])
    kbuf = jnp.concatenate([k_cache.reshape(b, WINDOW, KV_WIDTH), ka.reshape(b, t, KV_WIDTH)], axis=1)
    vbuf = jnp.concatenate([v_cache.reshape(b, WINDOW, KV_WIDTH), va.reshape(b, t, KV_WIDTH)], axis=1)
    seq = lambda a: a.reshape(b, t, a.shape[-1])
    mixed, s_new = _mixer(seq(qa), kbuf, vbuf, seq(qr), seq(kr), seq(vr), seq(gate), s0, p["sinks"], p["ret_g"],
                          *p["decay"], pos0)
    x1, xn, idx_t, g_t = _route(mixed.reshape(n, D_MODEL), x, p["w_out"], p["ffn_g"], p["wq_t"], p["subk"])
    idx = idx_t.T
    w = _peer_u(idx, xn.reshape(n, ROW_TILES, LANES), g_t.T, p["u_tab"])
    y = _peer_v(idx, w, x1.reshape(n, ROW_TILES, LANES), p["final_g"], p["v_tab"])
    new_k = kbuf[:, t:].reshape(b, WINDOW, KV_HEADS, HEAD_DIM)
    new_v = vbuf[:, t:].reshape(b, WINDOW, KV_HEADS, HEAD_DIM)
    return y.reshape(b, t, D_MODEL), new_k, new_v, s_new


def kernel(x_prompt, x_sample, cache_attn_k, cache_attn_v, state_ret, attn_norm_g, w_in, attn_sinks, ret_norm_g,
           w_out, ffn_norm_g, peer_w_q, peer_sub_keys, peer_u, peer_v, final_norm_g):
    assert attn_norm_g.shape[0] == 1, "single-layer trunk"
    p = dict(
        attn_g=attn_norm_g[0][None],
        w_in=w_in[0].astype(BF16),
        sinks=attn_sinks[0][None],
        ret_g=ret_norm_g[0][None],
        w_out=w_out[0].astype(BF16),
        ffn_g=ffn_norm_g[0][None],
        wq_t=peer_w_q[0].T.astype(BF16),
        subk=peer_sub_keys[0].reshape(2 * PEER_HEADS, PEER_KEYS, PEER_KEYS).astype(BF16),
        u_tab=_pack_table(peer_u[0]),
        v_tab=_pack_table(peer_v[0]),
        final_g=final_norm_g.reshape(ROW_TILES, LANES),
        decay=_decay_tables(),
    )
    bp = x_prompt.shape[0]
    zero_kv = jnp.zeros((bp, WINDOW, KV_HEADS, HEAD_DIM), F32)
    zero_s = jnp.zeros((bp, RET_HEADS, HEAD_DIM, HEAD_DIM), F32)
    yp, kp, vp, sp = _trunk(x_prompt, zero_kv, zero_kv, zero_s, 0, p)
    ys, ks, vs, ss = _trunk(x_sample, cache_attn_k[0], cache_attn_v[0], state_ret[0], PAST_LEN, p)
    return (yp, ys, kp[None], vp[None], sp[None], ks[None], vs[None], ss[None])
```

```python
import functools
import math

import numpy as np
import jax
import jax.numpy as jnp
from jax import lax
from jax.experimental import pallas as pl
from jax.experimental.pallas import tpu as pltpu

D_MODEL = 1024
CHUNK = 64
WINDOW = 128
HEAD_DIM = 64
ATTN_HEADS = 8
KV_HEADS = 2
ATTN_GROUP = ATTN_HEADS // KV_HEADS
RET_HEADS = 8
ATTN_WIDTH = ATTN_HEADS * HEAD_DIM
KV_WIDTH = KV_HEADS * HEAD_DIM
RET_WIDTH = RET_HEADS * HEAD_DIM
IN_SPLITS = (ATTN_WIDTH, KV_WIDTH, KV_WIDTH, RET_WIDTH, RET_WIDTH, RET_WIDTH, RET_WIDTH)
IN_WIDTH = sum(IN_SPLITS)
PEER_HEADS = 8
PEER_KEYS = 128
PEER_EXPERTS = PEER_KEYS * PEER_KEYS
PEER_TOPK = 16
PEER_PAIRS = PEER_HEADS * PEER_TOPK
EPS = 1e-6
NEG_INF = -1e30
PAST_LEN = 2048

LANES = 128
SUBLANES = 8
ROW_TILES = D_MODEL // LANES
VMEM_LIMIT = 48 * 1024 * 1024

TOKEN_TILE = 256
PEER_TILE = 128

BF16 = jnp.bfloat16
F32 = jnp.float32

ALIBI_SLOPES = tuple(float(2.0 ** (-8.0 * h / ATTN_HEADS)) for h in range(1, ATTN_HEADS + 1))
RET_LOG_DECAY = tuple(float(np.log(1.0 - 2.0 ** (-5.0 - h))) for h in range(RET_HEADS))


def _rms(x, g):
    return x * lax.rsqrt(jnp.mean(x * x, axis=-1, keepdims=True) + EPS) * g


def _params(*sem):
    return pltpu.CompilerParams(dimension_semantics=sem, vmem_limit_bytes=VMEM_LIMIT)


def _inproj_kernel(x_ref, g_ref, w_ref, *out_refs):
    h = _rms(x_ref[...], g_ref[...]).astype(BF16)
    proj = jnp.dot(h, w_ref[...], preferred_element_type=F32)
    off = 0
    for o_ref, width in zip(out_refs, IN_SPLITS):
        o_ref[...] = proj[:, off:off + width]
        off += width


def _inproj(x, g, w_bf16):
    n = x.shape[0]
    tm = TOKEN_TILE
    return pl.pallas_call(
        _inproj_kernel,
        grid=(n // tm,),
        in_specs=[
            pl.BlockSpec((tm, D_MODEL), lambda i: (i, 0)),
            pl.BlockSpec((1, D_MODEL), lambda i: (0, 0)),
            pl.BlockSpec((D_MODEL, IN_WIDTH), lambda i: (0, 0)),
        ],
        out_specs=[pl.BlockSpec((tm, w), lambda i: (i, 0)) for w in IN_SPLITS],
        out_shape=[jax.ShapeDtypeStruct((n, w), F32) for w in IN_SPLITS],
        compiler_params=_params("arbitrary"),
        name="inproj",
    )(x, g, w_bf16)


def _mixer_kernel(pos_off, nb, qa_ref, k0_ref, k1_ref, k2_ref, v0_ref, v1_ref, v2_ref, qr_ref, kr_ref, vr_ref,
                  gate_ref, s0_ref, sink_ref, rg_ref, dintra_ref, din_ref, dout_ref, mixed_ref, sfin_ref, s_ref):
    n = pl.program_id(1)

    @pl.when(n == 0)
    def _():
        s_ref[...] = s0_ref[0]

    kk = jnp.concatenate([k0_ref[0], k1_ref[0], k2_ref[0]], axis=0)
    vv = jnp.concatenate([v0_ref[0], v1_ref[0], v2_ref[0]], axis=0)
    qa = qa_ref[0]
    lk = WINDOW + CHUNK
    qi = lax.broadcasted_iota(jnp.int32, (CHUNK, lk), 0)
    kj = lax.broadcasted_iota(jnp.int32, (CHUNK, lk), 1)
    dist = jnp.abs(qi + WINDOW - kj).astype(F32)
    valid = (kj + (pos_off + n * CHUNK)) >= 0
    outs = []
    for kh in range(KV_HEADS):
        kmat = kk[:, kh * HEAD_DIM:(kh + 1) * HEAD_DIM].astype(BF16)
        vmat = vv[:, kh * HEAD_DIM:(kh + 1) * HEAD_DIM].astype(BF16)
        for g in range(ATTN_GROUP):
            h = kh * ATTN_GROUP + g
            q = qa[:, h * HEAD_DIM:(h + 1) * HEAD_DIM].astype(BF16)
            s = lax.dot_general(q, kmat, (((1,), (1,)), ((), ())), preferred_element_type=F32) * (HEAD_DIM ** -0.5)
            s = jnp.where(valid, s - ALIBI_SLOPES[h] * dist, NEG_INF)
            sink = sink_ref[0, h]
            m = jnp.maximum(jnp.max(s, axis=-1, keepdims=True), sink)
            p = jnp.exp(s - m)
            denom = jnp.sum(p, axis=-1, keepdims=True) + jnp.exp(sink - m)
            p = p / denom
            outs.append(jnp.dot(p.astype(BF16), vmat, preferred_element_type=F32))

    qr = qr_ref[0]
    kr = kr_ref[0]
    vr = vr_ref[0]
    gate = gate_ref[0]
    rg = rg_ref[...]
    for h in range(RET_HEADS):
        sl = slice(h * HEAD_DIM, (h + 1) * HEAD_DIM)
        q = qr[:, sl]
        k = kr[:, sl] * (HEAD_DIM ** -0.5)
        v = vr[:, sl].astype(BF16)
        s_prev = s_ref[h]
        scores = lax.dot_general(q.astype(BF16), k.astype(BF16), (((1,), (1,)), ((), ())), preferred_element_type=F32)
        scores = scores * dintra_ref[h]
        intra = jnp.dot(scores.astype(BF16), v, preferred_element_type=F32)
        kd = (k * dout_ref[h]).astype(BF16)
        kv = lax.dot_general(kd, v, (((0,), (0,)), ((), ())), preferred_element_type=F32)
        qd = (q * din_ref[h]).astype(BF16)
        cross = jnp.dot(qd, s_prev.astype(BF16), preferred_element_type=F32)
        s_ref[h] = math.exp(CHUNK * RET_LOG_DECAY[h]) * s_prev + kv
        o = intra + cross
        mu = jnp.mean(o, axis=-1, keepdims=True)
        oc = o - mu
        var = jnp.mean(oc * oc, axis=-1, keepdims=True)
        y = oc * lax.rsqrt(var + EPS) * rg[:, sl]
        gt = gate[:, sl]
        outs.append(y * (gt * jax.nn.sigmoid(gt)))
    mixed_ref[0] = jnp.concatenate(outs, axis=-1)

    @pl.when(n == nb - 1)
    def _():
        sfin_ref[0] = s_ref[...]


def _mixer(qa, kbuf, vbuf, qr, kr, vr, gate, s0, sinks, rg, dintra, din, dout, pos0):
    b, t, _ = qa.shape
    nb = t // CHUNK
    tok = lambda w: pl.BlockSpec((1, CHUNK, w), lambda i, j: (i, j, 0))
    kvs = [pl.BlockSpec((1, CHUNK, KV_WIDTH), functools.partial(lambda i, j, o: (i, j + o, 0), o=o)) for o in range(3)]
    state = pl.BlockSpec((1, RET_HEADS, HEAD_DIM, HEAD_DIM), lambda i, j: (i, 0, 0, 0))
    const3 = pl.BlockSpec((RET_HEADS, CHUNK, HEAD_DIM), lambda i, j: (0, 0, 0))
    return pl.pallas_call(
        functools.partial(_mixer_kernel, pos0 - WINDOW, nb),
        grid=(b, nb),
        in_specs=[tok(ATTN_WIDTH)] + kvs + kvs + [tok(RET_WIDTH)] * 4 + [
            state,
            pl.BlockSpec(memory_space=pltpu.SMEM),
            pl.BlockSpec((1, RET_WIDTH), lambda i, j: (0, 0)),
            const3, const3, const3,
        ],
        out_specs=[pl.BlockSpec((1, CHUNK, D_MODEL), lambda i, j: (i, j, 0)), state],
        out_shape=[
            jax.ShapeDtypeStruct((b, t, D_MODEL), F32),
            jax.ShapeDtypeStruct((b, RET_HEADS, HEAD_DIM, HEAD_DIM), F32),
        ],
        scratch_shapes=[pltpu.VMEM((RET_HEADS, HEAD_DIM, HEAD_DIM), F32)],
        compiler_params=_params("arbitrary", "arbitrary"),
        name="mixer",
    )(qa, kbuf, kbuf, kbuf, vbuf, vbuf, vbuf, qr, kr, vr, gate, s0, sinks, rg, dintra, din, dout)


def _decay_tables():
    lg = np.asarray(RET_LOG_DECAY, np.float32)
    i = np.arange(CHUNK, dtype=np.float32)
    dintra = np.exp(np.abs(i[:, None] - i[None, :])[None] * lg[:, None, None])
    din = np.broadcast_to(np.exp((i + 1.0)[None, :] * lg[:, None])[:, :, None], dintra.shape)
    dout = np.broadcast_to(np.exp((CHUNK - 1.0 - i)[None, :] * lg[:, None])[:, :, None], dintra.shape)
    return tuple(jnp.asarray(a, F32) for a in (dintra, din, dout))


def _topk_rows(s, k, payload=None):
    rows = s.shape[0]
    iota = lax.broadcasted_iota(jnp.int32, s.shape, 0)
    vals, idxs = [], []
    for _ in range(k):
        m = jnp.max(s, axis=0, keepdims=True)
        pos = jnp.min(jnp.where(s == m, iota, rows), axis=0, keepdims=True)
        hit = iota == pos
        vals.append(m)
        if payload is None:
            idxs.append(pos)
        else:
            idxs.append(jnp.max(jnp.where(hit, payload, -1), axis=0, keepdims=True))
        s = jnp.where(hit, -jnp.inf, s)
    return jnp.concatenate(vals, axis=0), jnp.concatenate(idxs, axis=0)


def _route_kernel(mixed_ref, x_ref, wout_ref, fg_ref, wqt_ref, subk_ref, x1_ref, xn_ref, idx_ref, gw_ref):
    x1 = x_ref[...] + jnp.dot(mixed_ref[...].astype(BF16), wout_ref[...], preferred_element_type=F32)
    x1_ref[...] = x1
    xn = _rms(x1, fg_ref[...])
    xn_ref[...] = xn
    qt = lax.dot_general(wqt_ref[...], xn.astype(BF16), (((1,), (1,)), ((), ())), preferred_element_type=F32)
    for h in range(PEER_HEADS):
        sv, si = [], []
        for c in range(2):
            hc = h * 2 + c
            qh = qt[hc * PEER_KEYS:(hc + 1) * PEER_KEYS, :].astype(BF16)
            s = jnp.dot(subk_ref[hc], qh, preferred_element_type=F32)
            v, i = _topk_rows(s, PEER_TOPK)
            sv.append(v)
            si.append(i)
        cand = jnp.concatenate([sv[0][a:a + 1, :] + sv[1] for a in range(PEER_TOPK)], axis=0)
        cidx = jnp.concatenate([si[0][a:a + 1, :] * PEER_KEYS + si[1] for a in range(PEER_TOPK)], axis=0)
        top_s, expert = _topk_rows(cand, PEER_TOPK, payload=cidx)
        e = jnp.exp(top_s - top_s[0:1, :])
        gw_ref[h * PEER_TOPK:(h + 1) * PEER_TOPK, :] = e / jnp.sum(e, axis=0, keepdims=True)
        idx_ref[h * PEER_TOPK:(h + 1) * PEER_TOPK, :] = expert


def _route(mixed, x, wout_bf16, fg, wqt_bf16, subk_bf16):
    n = x.shape[0]
    tm = TOKEN_TILE
    tok = pl.BlockSpec((tm, D_MODEL), lambda i: (i, 0))
    pair = pl.BlockSpec((PEER_PAIRS, tm), lambda i: (0, i))
    return pl.pallas_call(
        _route_kernel,
        grid=(n // tm,),
        in_specs=[
            tok, tok,
            pl.BlockSpec((D_MODEL, D_MODEL), lambda i: (0, 0)),
            pl.BlockSpec((1, D_MODEL), lambda i: (0, 0)),
            pl.BlockSpec((2 * PEER_HEADS * PEER_KEYS, D_MODEL), lambda i: (0, 0)),
            pl.BlockSpec((2 * PEER_HEADS, PEER_KEYS, PEER_KEYS), lambda i: (0, 0, 0)),
        ],
        out_specs=[tok, tok, pair, pair],
        out_shape=[
            jax.ShapeDtypeStruct((n, D_MODEL), F32),
            jax.ShapeDtypeStruct((n, D_MODEL), F32),
            jax.ShapeDtypeStruct((PEER_PAIRS, n), jnp.int32),
            jax.ShapeDtypeStruct((PEER_PAIRS, n), F32),
        ],
        compiler_params=_params("arbitrary"),
        name="route",
    )(mixed, x, wout_bf16, fg, wqt_bf16, subk_bf16)


def _expert_table(t):
    return t.astype(BF16).reshape(t.shape[0], ROW_TILES, LANES)


def _split3(x):
    hi = x.astype(BF16).astype(F32)
    r = x - hi
    mid = r.astype(BF16).astype(F32)
    lo = r - mid
    return jnp.concatenate([hi, mid, lo], axis=0).astype(BF16)


def _sum3(z):
    return (z[0:SUBLANES] + z[SUBLANES:2 * SUBLANES]) + z[2 * SUBLANES:3 * SUBLANES]


def _diag_mask():
    r = lax.broadcasted_iota(jnp.int32, (ROW_TILES, D_MODEL), 0)
    k = lax.broadcasted_iota(jnp.int32, (ROW_TILES, D_MODEL), 1)
    return (k & (ROW_TILES - 1)) == r


def _stage_rows(idx_ref, tab_ref, t, dst_ref):
    experts = idx_ref.at[pl.ds(t * PEER_PAIRS, PEER_PAIRS)]
    for k in range(PEER_PAIRS // 2):
        two = jnp.concatenate([tab_ref[experts[2 * k]], tab_ref[experts[2 * k + 1]]], axis=0)
        dst_ref[2 * ROW_TILES * k:2 * ROW_TILES * (k + 1), :] = two


def _one_token_behind(idx_ref, tab_ref, ga_ref, gb_ref, consume):
    _stage_rows(idx_ref, tab_ref, 0, ga_ref)

    def body(i, carry):
        t0 = 2 * i
        _stage_rows(idx_ref, tab_ref, t0 + 1, gb_ref)
        consume(t0, ga_ref)
        _stage_rows(idx_ref, tab_ref, jnp.minimum(t0 + 2, PEER_TILE - 1), ga_ref)
        consume(t0 + 1, gb_ref)
        return carry

    lax.fori_loop(0, PEER_TILE // 2, body, 0)


def _peer_u_kernel(idx_ref, xn_ref, g_ref, sel_ref, tab_ref, w_ref, ga_ref, gb_ref, z_ref):
    diag = _diag_mask()

    def consume(t, staged_ref):
        rows = staged_ref[...]
        z = lax.dot_general(_split3(xn_ref[t]), rows, (((1,), (1,)), ((), ())), preferred_element_type=F32)
        z_ref[pl.ds(t, 1), :] = jnp.sum(jnp.where(diag, _sum3(z), 0.0), axis=0, keepdims=True)

    _one_token_behind(idx_ref, tab_ref, ga_ref, gb_ref, consume)
    zs = z_ref[...]
    hi = zs.astype(BF16)
    r1 = zs - hi.astype(F32)
    mid = r1.astype(BF16)
    lo = (r1 - mid.astype(F32)).astype(BF16)
    sel = sel_ref[...]
    hid = (jnp.dot(hi, sel, preferred_element_type=F32) + jnp.dot(mid, sel, preferred_element_type=F32)
           + jnp.dot(lo, sel, preferred_element_type=F32))
    w_ref[...] = g_ref[...] * (0.5 * hid * (1.0 + lax.erf(hid * math.sqrt(0.5))))


def _peer_specs():
    offs = pl.BlockSpec((PEER_TILE * PEER_PAIRS,), lambda i: (i,), memory_space=pltpu.SMEM)
    row = pl.BlockSpec((PEER_TILE, ROW_TILES, LANES), lambda i: (i, 0, 0))
    table = pl.BlockSpec((PEER_EXPERTS, ROW_TILES, LANES), lambda i: (0, 0, 0), pipeline_mode=pl.Buffered(1))
    staging = pltpu.VMEM((PEER_PAIRS * ROW_TILES, LANES), BF16)
    return offs, row, table, staging


def _peer_u(offs, xn3, g, sel, tab):
    n = xn3.shape[0]
    offs_spec, row, table, staging = _peer_specs()
    pair = pl.BlockSpec((PEER_TILE, PEER_PAIRS), lambda i: (i, 0))
    return pl.pallas_call(
        _peer_u_kernel,
        grid=(n // PEER_TILE,),
        in_specs=[offs_spec, row, pair, pl.BlockSpec((D_MODEL, PEER_PAIRS), lambda i: (0, 0)), table],
        out_specs=pair,
        out_shape=jax.ShapeDtypeStruct((n, PEER_PAIRS), F32),
        scratch_shapes=[staging, staging, pltpu.VMEM((PEER_TILE, D_MODEL), F32)],
        compiler_params=_params("arbitrary"),
        name="peer_u",
    )(offs, xn3, g, sel, tab)


def _peer_v_kernel(idx_ref, w8_ref, x1_ref, fg_ref, tab_ref, y_ref, ga_ref, gb_ref):
    diag = _diag_mask()

    def consume(t, staged_ref):
        rows = staged_ref[...]
        wsel =jnp.where(diag, jnp.broadcast_to(w8_ref[t], (ROW_TILES, D_MODEL)), 0.0)
        o = jnp.dot(_split3(wsel), rows, preferred_element_type=F32)
        y_ref[t] = x1_ref[t] + _sum3(o)

    _one_token_behind(idx_ref, tab_ref, ga_ref, gb_ref, consume)
    x2 = y_ref[...]
    ms = jnp.sum(jnp.sum(x2 * x2, axis=2, keepdims=True), axis=1, keepdims=True) * (1.0 / D_MODEL)
    y_ref[...] = x2 * lax.rsqrt(ms + EPS) * fg_ref[...]


def _peer_v(offs, w8, x13, fg, tab):
    n = x13.shape[0]
    offs_spec, row, table, staging = _peer_specs()
    return pl.pallas_call(
        _peer_v_kernel,
        grid=(n // PEER_TILE,),
        in_specs=[offs_spec, pl.BlockSpec((PEER_TILE, 1, D_MODEL), lambda i: (i, 0, 0)), row,
                  pl.BlockSpec((ROW_TILES, LANES), lambda i: (0, 0)), table],
        out_specs=row,
        out_shape=jax.ShapeDtypeStruct((n, ROW_TILES, LANES), F32),
        scratch_shapes=[staging, staging],
        compiler_params=_params("arbitrary"),
        name="peer_v",
    )(offs, w8, x13, fg, tab)


def _trunk(x3, k_cache, v_cache, s0, pos0, p):
    b, t, _ = x3.shape
    n = b * t
    x = x3.reshape(n, D_MODEL)
    qa, ka, va, qr, kr, vr, gate = _inproj(x, p["attn_g"], p["w_in"])
    kbuf = jnp.concatenate([k_cache.reshape(b, WINDOW, KV_WIDTH), ka.reshape(b, t, KV_WIDTH)], axis=1)
    vbuf = jnp.concatenate([v_cache.reshape(b, WINDOW, KV_WIDTH), va.reshape(b, t, KV_WIDTH)], axis=1)
    seq = lambda a: a.reshape(b, t, a.shape[-1])
    mixed, s_new = _mixer(seq(qa), kbuf, vbuf, seq(qr), seq(kr), seq(vr), seq(gate), s0, p["sinks"], p["ret_g"],
                          *p["decay"], pos0)
    x1, xn, idx_t, g_t = _route(mixed.reshape(n, D_MODEL), x, p["w_out"], p["ffn_g"], p["wq_t"], p["subk"])
    offs = idx_t.T.reshape(n * PEER_PAIRS)
    w = _peer_u(offs, xn.reshape(n, ROW_TILES, LANES), g_t.T, p["pair_sum"], p["u_tab"])
    w8 = jnp.repeat(w, ROW_TILES, axis=1).reshape(n, 1, D_MODEL)
    y = _peer_v(offs, w8, x1.reshape(n, ROW_TILES, LANES), p["final_g"], p["v_tab"])
    new_k = kbuf[:, t:].reshape(b, WINDOW, KV_HEADS, HEAD_DIM)
    new_v = vbuf[:, t:].reshape(b, WINDOW, KV_HEADS, HEAD_DIM)
    return y.reshape(b, t, D_MODEL), new_k, new_v, s_new


def kernel(x_prompt, x_sample, cache_attn_k, cache_attn_v, state_ret, attn_norm_g, w_in, attn_sinks, ret_norm_g,
           w_out, ffn_norm_g, peer_w_q, peer_sub_keys, peer_u, peer_v, final_norm_g):
    assert attn_norm_g.shape[0] == 1, "single-layer trunk"
    p = dict(
        attn_g=attn_norm_g[0][None],
        w_in=w_in[0].astype(BF16),
        sinks=attn_sinks[0][None],
        ret_g=ret_norm_g[0][None],
        w_out=w_out[0].astype(BF16),
        ffn_g=ffn_norm_g[0][None],
        wq_t=peer_w_q[0].T.astype(BF16),
        subk=peer_sub_keys[0].reshape(2 * PEER_HEADS, PEER_KEYS, PEER_KEYS).astype(BF16),
        u_tab=_expert_table(peer_u[0]),
        v_tab=_expert_table(peer_v[0]),
        final_g=final_norm_g.reshape(ROW_TILES, LANES),
        decay=_decay_tables(),
        pair_sum=jnp.asarray(np.arange(D_MODEL)[:, None] // ROW_TILES == np.arange(PEER_PAIRS)[None, :], BF16),
    )
    bp = x_prompt.shape[0]
    zero_kv = jnp.zeros((bp, WINDOW, KV_HEADS, HEAD_DIM), F32)
    zero_s = jnp.zeros((bp, RET_HEADS, HEAD_DIM, HEAD_DIM), F32)
    yp, kp, vp, sp = _trunk(x_prompt, zero_kv, zero_kv, zero_s, 0, p)
    ys, ks, vs, ss = _trunk(x_sample, cache_attn_k[0], cache_attn_v[0], state_ret[0], PAST_LEN, p)
    return (yp, ys, kp[None], vp[None], sp[None], ks[None], vs[None], ss[None])
```

```python
import functools
import math

import numpy as np
import jax
import jax.numpy as jnp
from jax import lax
from jax.experimental import pallas as pl
from jax.experimental.pallas import tpu as pltpu

D_MODEL = 1024
CHUNK = 64
WINDOW = 128
HEAD_DIM = 64
ATTN_HEADS = 8
KV_HEADS = 2
ATTN_GROUP = ATTN_HEADS // KV_HEADS
RET_HEADS = 8
ATTN_WIDTH = ATTN_HEADS * HEAD_DIM
KV_WIDTH = KV_HEADS * HEAD_DIM
RET_WIDTH = RET_HEADS * HEAD_DIM
IN_SPLITS = (ATTN_WIDTH, KV_WIDTH, KV_WIDTH, RET_WIDTH, RET_WIDTH, RET_WIDTH, RET_WIDTH)
IN_WIDTH = sum(IN_SPLITS)
PEER_HEADS = 8
PEER_KEYS = 128
PEER_EXPERTS = PEER_KEYS * PEER_KEYS
PEER_TOPK = 16
PEER_PAIRS = PEER_HEADS * PEER_TOPK
EPS = 1e-6
NEG_INF = -1e30
PAST_LEN = 2048

LANES = 128
SUBLANES = 8
ROW_TILES = D_MODEL // LANES
VMEM_LIMIT = 48 * 1024 * 1024

TOKEN_TILE = 256
PEER_TILE = 128

BF16 = jnp.bfloat16
F32 = jnp.float32

ALIBI_SLOPES = tuple(float(2.0 ** (-8.0 * h / ATTN_HEADS)) for h in range(1, ATTN_HEADS + 1))
RET_LOG_DECAY = tuple(float(np.log(1.0 - 2.0 ** (-5.0 - h))) for h in range(RET_HEADS))


def _rms(x, g):
    return x * lax.rsqrt(jnp.mean(x * x, axis=-1, keepdims=True) + EPS) * g


def _params(*sem):
    return pltpu.CompilerParams(dimension_semantics=sem, vmem_limit_bytes=VMEM_LIMIT)


def _inproj_kernel(x_ref, g_ref, w_ref, *out_refs):
    h = _rms(x_ref[...], g_ref[...]).astype(BF16)
    proj = jnp.dot(h, w_ref[...], preferred_element_type=F32)
    off = 0
    for o_ref, width in zip(out_refs, IN_SPLITS):
        o_ref[...] = proj[:, off:off + width]
        off += width


def _inproj(x, g, w_bf16):
    n = x.shape[0]
    tm = TOKEN_TILE
    return pl.pallas_call(
        _inproj_kernel,
        grid=(n // tm,),
        in_specs=[
            pl.BlockSpec((tm, D_MODEL), lambda i: (i, 0)),
            pl.BlockSpec((1, D_MODEL), lambda i: (0, 0)),
            pl.BlockSpec((D_MODEL, IN_WIDTH), lambda i: (0, 0)),
        ],
        out_specs=[pl.BlockSpec((tm, w), lambda i: (i, 0)) for w in IN_SPLITS],
        out_shape=[jax.ShapeDtypeStruct((n, w), F32) for w in IN_SPLITS],
        compiler_params=_params("arbitrary"),
        name="inproj",
    )(x, g, w_bf16)


def _mixer_kernel(pos_off, nb, qa_ref, k0_ref, k1_ref, k2_ref, v0_ref, v1_ref, v2_ref, qr_ref, kr_ref, vr_ref,
                  gate_ref, s0_ref, sink_ref, rg_ref, dintra_ref, din_ref, dout_ref, mixed_ref, sfin_ref, s_ref):
    n = pl.program_id(1)

    @pl.when(n == 0)
    def _():
        s_ref[...] = s0_ref[0]

    kk = jnp.concatenate([k0_ref[0], k1_ref[0], k2_ref[0]], axis=0)
    vv = jnp.concatenate([v0_ref[0], v1_ref[0], v2_ref[0]], axis=0)
    qa = qa_ref[0]
    qr = qr_ref[0]
    kr = kr_ref[0] * (HEAD_DIM ** -0.5)
    vr = vr_ref[0]
    gate = gate_ref[0]
    rg = rg_ref[...]
    lk = WINDOW + CHUNK
    rows = ATTN_GROUP * CHUNK
    qi = lax.broadcasted_iota(jnp.int32, (rows, lk), 0)
    kj = lax.broadcasted_iota(jnp.int32, (rows, lk), 1)
    dist = jnp.abs((qi & (CHUNK - 1)) + WINDOW - kj).astype(F32)
    valid = (kj + (pos_off + n * CHUNK)) >= 0
    grp = qi // CHUNK

    def head(a, h):
        return a[:, h * HEAD_DIM:(h + 1) * HEAD_DIM]

    att_s = []
    for kh in range(KV_HEADS):
        q4 = jnp.concatenate([head(qa, kh * ATTN_GROUP + g) for g in range(ATTN_GROUP)], axis=0).astype(BF16)
        att_s.append(lax.dot_general(q4, head(kk, kh).astype(BF16), (((1,), (1,)), ((), ())),
                                     preferred_element_type=F32))
    ret_sc, ret_kv, ret_cross, s_prev = [], [], [], []
    for h in range(RET_HEADS):
        q, k, v = head(qr, h), head(kr, h), head(vr, h).astype(BF16)
        s_prev.append(s_ref[h])
        ret_sc.append(lax.dot_general(q.astype(BF16), k.astype(BF16), (((1,), (1,)), ((), ())),
                                      preferred_element_type=F32))
        ret_kv.append(lax.dot_general((k * dout_ref[h]).astype(BF16), v, (((0,), (0,)), ((), ())),
                                      preferred_element_type=F32))
        ret_cross.append(jnp.dot((q * din_ref[h]).astype(BF16), s_prev[h].astype(BF16), preferred_element_type=F32))

    outs = []
    for kh in range(KV_HEADS):
        slope = jnp.zeros((rows, lk), F32)
        sink = jnp.zeros((rows, 1), F32)
        for g in range(ATTN_GROUP):
            h = kh * ATTN_GROUP + g
            slope = jnp.where(grp == g, ALIBI_SLOPES[h], slope)
            sink = jnp.where(grp[:, 0:1] == g, sink_ref[0, h], sink)
        s = jnp.where(valid, att_s[kh] * (HEAD_DIM ** -0.5) - slope * dist, NEG_INF)
        m = jnp.maximum(jnp.max(s, axis=-1, keepdims=True), sink)
        p = jnp.exp(s - m)
        denom = jnp.sum(p, axis=-1, keepdims=True) + jnp.exp(sink - m)
        o4 = jnp.dot((p / denom).astype(BF16), head(vv, kh).astype(BF16), preferred_element_type=F32)
        outs += [o4[g * CHUNK:(g + 1) * CHUNK] for g in range(ATTN_GROUP)]
    for h in range(RET_HEADS):
        intra = jnp.dot((ret_sc[h] * dintra_ref[h]).astype(BF16), head(vr, h).astype(BF16),
                        preferred_element_type=F32)
        s_ref[h] = math.exp(CHUNK * RET_LOG_DECAY[h]) * s_prev[h] + ret_kv[h]
        o = intra + ret_cross[h]
        mu = jnp.mean(o, axis=-1, keepdims=True)
        oc = o - mu
        var = jnp.mean(oc * oc, axis=-1, keepdims=True)
        gt = head(gate, h)
        outs.append(oc * lax.rsqrt(var + EPS) * head(rg, h) * (gt * jax.nn.sigmoid(gt)))
    mixed_ref[0] = jnp.concatenate(outs, axis=-1)

    @pl.when(n == nb - 1)
    def _():
        sfin_ref[0] = s_ref[...]


def _mixer(qa, kbuf, vbuf, qr, kr, vr, gate, s0, sinks, rg, dintra, din, dout, pos0):
    b, t, _ = qa.shape
    nb = t // CHUNK
    tok = lambda w: pl.BlockSpec((1, CHUNK, w), lambda i, j: (i, j, 0))
    kvs = [pl.BlockSpec((1, CHUNK, KV_WIDTH), functools.partial(lambda i, j, o: (i, j + o, 0), o=o)) for o in range(3)]
    state = pl.BlockSpec((1, RET_HEADS, HEAD_DIM, HEAD_DIM), lambda i, j: (i, 0, 0, 0))
    const3 = pl.BlockSpec((RET_HEADS, CHUNK, HEAD_DIM), lambda i, j: (0, 0, 0))
    return pl.pallas_call(
        functools.partial(_mixer_kernel, pos0 - WINDOW, nb),
        grid=(b, nb),
        in_specs=[tok(ATTN_WIDTH)] + kvs + kvs + [tok(RET_WIDTH)] * 4 + [
            state,
            pl.BlockSpec(memory_space=pltpu.SMEM),
            pl.BlockSpec((1, RET_WIDTH), lambda i, j: (0, 0)),
            const3, const3, const3,
        ],
        out_specs=[pl.BlockSpec((1, CHUNK, D_MODEL), lambda i, j: (i, j, 0)), state],
        out_shape=[
            jax.ShapeDtypeStruct((b, t, D_MODEL), F32),
            jax.ShapeDtypeStruct((b, RET_HEADS, HEAD_DIM, HEAD_DIM), F32),
        ],
        scratch_shapes=[pltpu.VMEM((RET_HEADS, HEAD_DIM, HEAD_DIM), F32)],
        compiler_params=_params("arbitrary", "arbitrary"),
        name="mixer",
    )(qa, kbuf, kbuf, kbuf, vbuf, vbuf, vbuf, qr, kr, vr, gate, s0, sinks, rg, dintra, din, dout)


def _decay_tables():
    lg = np.asarray(RET_LOG_DECAY, np.float32)
    i = np.arange(CHUNK, dtype=np.float32)
    dintra = np.exp(np.abs(i[:, None] - i[None, :])[None] * lg[:, None, None])
    din = np.broadcast_to(np.exp((i + 1.0)[None, :] * lg[:, None])[:, :, None], dintra.shape)
    dout = np.broadcast_to(np.exp((CHUNK - 1.0 - i)[None, :] * lg[:, None])[:, :, None], dintra.shape)
    return tuple(jnp.asarray(a, F32) for a in (dintra, din, dout))


def _topk_rows(s, k, payload=None):
    rows = s.shape[0]
    iota = lax.broadcasted_iota(jnp.int32, s.shape, 0)
    vals, idxs = [], []
    for _ in range(k):
        m = jnp.max(s, axis=0, keepdims=True)
        pos = jnp.min(jnp.where(s == m, iota, rows), axis=0, keepdims=True)
        hit = iota == pos
        vals.append(m)
        if payload is None:
            idxs.append(pos)
        else:
            idxs.append(jnp.max(jnp.where(hit, payload, -1), axis=0, keepdims=True))
        s = jnp.where(hit, -jnp.inf, s)
    return jnp.concatenate(vals, axis=0), jnp.concatenate(idxs, axis=0)


def _pruned_candidates(sv, si):
    k = PEER_TOPK
    row = lax.broadcasted_iota(jnp.int32, (SUBLANES, sv[0].shape[1]), 0)
    sums = [sv[0][0:1, :] + sv[1]]
    idxs = [si[0][0:1, :] * PEER_KEYS + si[1]]
    for a in range(1, SUBLANES):
        s = sv[0][a:a + 1, :] + sv[1][0:SUBLANES, :]
        sums.append(jnp.where(row < k // (a + 1), s, -jnp.inf))
        idxs.append(si[0][a:a + 1, :] * PEER_KEYS + si[1][0:SUBLANES, :])
    sums.append(sv[0][SUBLANES:k, :] + sv[1][0:1, :])
    idxs.append(si[0][SUBLANES:k, :] * PEER_KEYS + si[1][0:1, :])
    return jnp.concatenate(sums, axis=0), jnp.concatenate(idxs, axis=0)


def _route_kernel(mixed_ref, x_ref, wout_ref, fg_ref, wqt_ref, subk_ref, x1_ref, xn_ref, idx_ref, gw_ref):
    x1 = x_ref[...] + jnp.dot(mixed_ref[...].astype(BF16), wout_ref[...], preferred_element_type=F32)
    x1_ref[...] = x1
    xn = _rms(x1, fg_ref[...])
    xn_ref[...] = xn
    qt = lax.dot_general(wqt_ref[...], xn.astype(BF16), (((1,), (1,)), ((), ())), preferred_element_type=F32)
    for h in range(PEER_HEADS):
        sv, si = [], []
        for c in range(2):
            hc = h * 2 + c
            qh = qt[hc * PEER_KEYS:(hc + 1) * PEER_KEYS, :].astype(BF16)
            s = jnp.dot(subk_ref[hc], qh, preferred_element_type=F32)
            v, i = _topk_rows(s, PEER_TOPK)
            sv.append(v)
            si.append(i)
        cand, cidx = _pruned_candidates(sv, si)
        top_s, expert = _topk_rows(cand, PEER_TOPK, payload=cidx)
        e = jnp.exp(top_s - top_s[0:1, :])
        gw_ref[h * PEER_TOPK:(h + 1) * PEER_TOPK, :] = e / jnp.sum(e, axis=0, keepdims=True)
        idx_ref[h * PEER_TOPK:(h + 1) * PEER_TOPK, :] = expert


def _route(mixed, x, wout_bf16, fg, wqt_bf16, subk_bf16):
    n = x.shape[0]
    tm = TOKEN_TILE
    tok = pl.BlockSpec((tm, D_MODEL), lambda i: (i, 0))
    pair = pl.BlockSpec((PEER_PAIRS, tm), lambda i: (0, i))
    return pl.pallas_call(
        _route_kernel,
        grid=(n // tm,),
        in_specs=[
            tok, tok,
            pl.BlockSpec((D_MODEL, D_MODEL), lambda i: (0, 0)),
            pl.BlockSpec((1, D_MODEL), lambda i: (0, 0)),
            pl.BlockSpec((2 * PEER_HEADS * PEER_KEYS, D_MODEL), lambda i: (0, 0)),
            pl.BlockSpec((2 * PEER_HEADS, PEER_KEYS, PEER_KEYS), lambda i: (0, 0, 0)),
        ],
        out_specs=[tok, tok, pair, pair],
        out_shape=[
            jax.ShapeDtypeStruct((n, D_MODEL), F32),
            jax.ShapeDtypeStruct((n, D_MODEL), F32),
            jax.ShapeDtypeStruct((PEER_PAIRS, n), jnp.int32),
            jax.ShapeDtypeStruct((PEER_PAIRS, n), F32),
        ],
        compiler_params=_params("arbitrary"),
        name="route",
    )(mixed, x, wout_bf16, fg, wqt_bf16, subk_bf16)


def _expert_table(t):
    return t.astype(BF16).reshape(t.shape[0], ROW_TILES, LANES)


def _split3(x):
    hi = x.astype(BF16).astype(F32)
    r = x - hi
    mid = r.astype(BF16).astype(F32)
    lo = r - mid
    return jnp.concatenate([hi, mid, lo], axis=0).astype(BF16)


def _sum3(z):
    return (z[0:SUBLANES] + z[SUBLANES:2 * SUBLANES]) + z[2 * SUBLANES:3 * SUBLANES]


def _diag_mask():
    r = lax.broadcasted_iota(jnp.int32, (ROW_TILES, D_MODEL), 0)
    k = lax.broadcasted_iota(jnp.int32, (ROW_TILES, D_MODEL), 1)
    return (k & (ROW_TILES - 1)) == r


def _stage_rows(idx_ref, tab_ref, t, dst_ref):
    experts = idx_ref.at[pl.ds(t * PEER_PAIRS, PEER_PAIRS)]
    for p in range(PEER_PAIRS):
        dst_ref[ROW_TILES * p:ROW_TILES * (p + 1), :] = tab_ref[experts[p]]


def _one_token_behind(idx_ref, tab_ref, ga_ref, gb_ref, consume):
    _stage_rows(idx_ref, tab_ref, 0, ga_ref)

    def body(i, carry):
        t0 = 2 * i
        _stage_rows(idx_ref, tab_ref, t0 + 1, gb_ref)
        consume(t0, ga_ref)
        _stage_rows(idx_ref, tab_ref, jnp.minimum(t0 + 2, PEER_TILE - 1), ga_ref)
        consume(t0 + 1, gb_ref)
        return carry

    lax.fori_loop(0, PEER_TILE // 2, body, 0)


def _peer_u_kernel(idx_ref, xn_ref, g_ref, sel_ref, tab_ref, w_ref, ga_ref, gb_ref, z_ref):
    diag = _diag_mask()

    def consume(t, staged_ref):
        rows = staged_ref[...]
        z = lax.dot_general(_split3(xn_ref[t]), rows, (((1,), (1,)), ((), ())), preferred_element_type=F32)
        z_ref[pl.ds(t, 1), :] = jnp.sum(jnp.where(diag, _sum3(z), 0.0), axis=0, keepdims=True)

    _one_token_behind(idx_ref, tab_ref, ga_ref, gb_ref, consume)
    zs = z_ref[...]
    hi = zs.astype(BF16)
    r1 = zs - hi.astype(F32)
    mid = r1.astype(BF16)
    lo = (r1 - mid.astype(F32)).astype(BF16)
    sel = sel_ref[...]
    hid = (jnp.dot(hi, sel, preferred_element_type=F32) + jnp.dot(mid, sel, preferred_element_type=F32)
           + jnp.dot(lo, sel, preferred_element_type=F32))
    w_ref[...] = g_ref[...] * (0.5 * hid * (1.0 + lax.erf(hid * math.sqrt(0.5))))


def _peer_specs():
    offs = pl.BlockSpec((PEER_TILE * PEER_PAIRS,), lambda i: (i,), memory_space=pltpu.SMEM)
    row = pl.BlockSpec((PEER_TILE, ROW_TILES, LANES), lambda i: (i, 0, 0))
    table = pl.BlockSpec((PEER_EXPERTS, ROW_TILES, LANES), lambda i: (0, 0, 0), pipeline_mode=pl.Buffered(1))
    staging = pltpu.VMEM((PEER_PAIRS * ROW_TILES, LANES), BF16)
    return offs, row, table, staging


def _peer_u(offs, xn3, g, sel, tab):
    n = xn3.shape[0]
    offs_spec, row, table, staging = _peer_specs()
    pair = pl.BlockSpec((PEER_TILE, PEER_PAIRS), lambda i: (i, 0))
    return pl.pallas_call(
        _peer_u_kernel,
        grid=(n // PEER_TILE,),
        in_specs=[offs_spec, row, pair, pl.BlockSpec((D_MODEL, PEER_PAIRS), lambda i: (0, 0)), table],
        out_specs=pair,
        out_shape=jax.ShapeDtypeStruct((n, PEER_PAIRS), F32),
        scratch_shapes=[staging, staging, pltpu.VMEM((PEER_TILE, D_MODEL), F32)],
        compiler_params=_params("arbitrary"),
        name="peer_u",
    )(offs, xn3, g, sel, tab)


def _peer_v_kernel(idx_ref, w8_ref, x1_ref, fg_ref, tab_ref, y_ref, ga_ref, gb_ref):
    diag = _diag_mask()

    def consume(t, staged_ref):
        rows = staged_ref[...]
        wsel =jnp.where(diag, jnp.broadcast_to(w8_ref[t], (ROW_TILES, D_MODEL)), 0.0)
        o = jnp.dot(_split3(wsel), rows, preferred_element_type=F32)
        y_ref[t] = x1_ref[t] + _sum3(o)

    _one_token_behind(idx_ref, tab_ref, ga_ref, gb_ref, consume)
    x2 = y_ref[...]
    ms = jnp.sum(jnp.sum(x2 * x2, axis=2, keepdims=True), axis=1, keepdims=True) * (1.0 / D_MODEL)
    y_ref[...] = x2 * lax.rsqrt(ms + EPS) * fg_ref[...]


def _peer_v(offs, w8, x13, fg, tab):
    n = x13.shape[0]
    offs_spec, row, table, staging = _peer_specs()
    return pl.pallas_call(
        _peer_v_kernel,
        grid=(n // PEER_TILE,),
        in_specs=[offs_spec, pl.BlockSpec((PEER_TILE, 1, D_MODEL), lambda i: (i, 0, 0)), row,
                  pl.BlockSpec((ROW_TILES, LANES), lambda i: (0, 0)), table],
        out_specs=row,
        out_shape=jax.ShapeDtypeStruct((n, ROW_TILES, LANES), F32),
        scratch_shapes=[staging, staging],
        compiler_params=_params("arbitrary"),
        name="peer_v",
    )(offs, w8, x13, fg, tab)


def _trunk(x3, k_cache, v_cache, s0, pos0, p):
    b, t, _ = x3.shape
    n = b * t
    assert t % CHUNK == 0 and n % TOKEN_TILE == 0 and n % PEER_TILE == 0, (b, t)
    x = x3.reshape(n, D_MODEL)
    qa, ka, va, qr, kr, vr, gate = _inproj(x, p["attn_g"], p["w_in"])
    kbuf = jnp.concatenate([k_cache.reshape(b, WINDOW, KV_WIDTH), ka.reshape(b, t, KV_WIDTH)], axis=1)
    vbuf = jnp.concatenate([v_cache.reshape(b, WINDOW, KV_WIDTH), va.reshape(b, t, KV_WIDTH)], axis=1)
    seq = lambda a: a.reshape(b, t, a.shape[-1])
    mixed, s_new = _mixer(seq(qa), kbuf, vbuf, seq(qr), seq(kr), seq(vr), seq(gate), s0, p["sinks"], p["ret_g"],
                          *p["decay"], pos0)
    x1, xn, idx_t, g_t = _route(mixed.reshape(n, D_MODEL), x, p["w_out"], p["ffn_g"], p["wq_t"], p["subk"])
    offs = idx_t.T.reshape(n * PEER_PAIRS)
    w = _peer_u(offs, xn.reshape(n, ROW_TILES, LANES), g_t.T, p["pair_sum"], p["u_tab"])
    w8 = jnp.repeat(w, ROW_TILES, axis=1).reshape(n, 1, D_MODEL)
    y = _peer_v(offs, w8, x1.reshape(n, ROW_TILES, LANES), p["final_g"], p["v_tab"])
    new_k = kbuf[:, t:].reshape(b, WINDOW, KV_HEADS, HEAD_DIM)
    new_v = vbuf[:, t:].reshape(b, WINDOW, KV_HEADS, HEAD_DIM)
    return y.reshape(b, t, D_MODEL), new_k, new_v, s_new


def kernel(x_prompt, x_sample, cache_attn_k, cache_attn_v, state_ret, attn_norm_g, w_in, attn_sinks, ret_norm_g,
           w_out, ffn_norm_g, peer_w_q, peer_sub_keys, peer_u, peer_v, final_norm_g):
    assert attn_norm_g.shape[0] == 1, "single-layer trunk"
    p = dict(
        attn_g=attn_norm_g[0][None],
        w_in=w_in[0].astype(BF16),
        sinks=attn_sinks[0][None],
        ret_g=ret_norm_g[0][None],
        w_out=w_out[0].astype(BF16),
        ffn_g=ffn_norm_g[0][None],
        wq_t=peer_w_q[0].T.astype(BF16),
        subk=peer_sub_keys[0].reshape(2 * PEER_HEADS, PEER_KEYS, PEER_KEYS).astype(BF16),
        u_tab=_expert_table(peer_u[0]),
        v_tab=_expert_table(peer_v[0]),
        final_g=final_norm_g.reshape(ROW_TILES, LANES),
        decay=_decay_tables(),
        pair_sum=jnp.asarray(np.arange(D_MODEL)[:, None] // ROW_TILES == np.arange(PEER_PAIRS)[None, :], BF16),
    )
    bp = x_prompt.shape[0]
    zero_kv = jnp.zeros((bp, WINDOW, KV_HEADS, HEAD_DIM), F32)
    zero_s = jnp.zeros((bp, RET_HEADS, HEAD_DIM, HEAD_DIM), F32)
    yp, kp, vp, sp = _trunk(x_prompt, zero_kv, zero_kv, zero_s, 0, p)
    ys, ks, vs, ss = _trunk(x_sample, cache_attn_k[0], cache_attn_v[0], state_ret[0], PAST_LEN, p)
    return (yp, ys, kp[None], vp[None], sp[None], ks[None], vs[None], ss[None])
```

```python
import functools
import math

import numpy as np
import jax
import jax.numpy as jnp
from jax import lax
from jax.experimental import pallas as pl
from jax.experimental.pallas import tpu as pltpu

D_MODEL = 1024
CHUNK = 64
WINDOW = 128
HEAD_DIM = 64
ATTN_HEADS = 8
KV_HEADS = 2
ATTN_GROUP = ATTN_HEADS // KV_HEADS
RET_HEADS = 8
ATTN_WIDTH = ATTN_HEADS * HEAD_DIM
KV_WIDTH = KV_HEADS * HEAD_DIM
RET_WIDTH = RET_HEADS * HEAD_DIM
IN_SPLITS = (ATTN_WIDTH, KV_WIDTH, KV_WIDTH, RET_WIDTH, RET_WIDTH, RET_WIDTH, RET_WIDTH)
IN_WIDTH = sum(IN_SPLITS)
PEER_HEADS = 8
PEER_KEYS = 128
PEER_EXPERTS = PEER_KEYS * PEER_KEYS
PEER_TOPK = 16
PEER_PAIRS = PEER_HEADS * PEER_TOPK
EPS = 1e-6
NEG_INF = -1e30
PAST_LEN = 2048

LANES = 128
SUBLANES = 8
ROW_TILES = D_MODEL // LANES
VMEM_LIMIT = 48 * 1024 * 1024

TOKEN_TILE = 256
PEER_TILE = 128
STAGE_BUFFERS = 8

BF16 = jnp.bfloat16
F32 = jnp.float32

ALIBI_SLOPES = tuple(float(2.0 ** (-8.0 * h / ATTN_HEADS)) for h in range(1, ATTN_HEADS + 1))
RET_LOG_DECAY = tuple(float(np.log(1.0 - 2.0 ** (-5.0 - h))) for h in range(RET_HEADS))


def _rms(x, g):
    return x * lax.rsqrt(jnp.mean(x * x, axis=-1, keepdims=True) + EPS) * g


def _params(*sem):
    return pltpu.CompilerParams(dimension_semantics=sem, vmem_limit_bytes=VMEM_LIMIT)


def _inproj_kernel(x_ref, g_ref, w_ref, *out_refs):
    h = _rms(x_ref[...], g_ref[...]).astype(BF16)
    proj = jnp.dot(h, w_ref[...], preferred_element_type=F32)
    off = 0
    for o_ref, width in zip(out_refs, IN_SPLITS):
        o_ref[...] = proj[:, off:off + width]
        off += width


def _inproj(x, g, w_bf16):
    n = x.shape[0]
    tm = TOKEN_TILE
    return pl.pallas_call(
        _inproj_kernel,
        grid=(n // tm,),
        in_specs=[
            pl.BlockSpec((tm, D_MODEL), lambda i: (i, 0)),
            pl.BlockSpec((1, D_MODEL), lambda i: (0, 0)),
            pl.BlockSpec((D_MODEL, IN_WIDTH), lambda i: (0, 0)),
        ],
        out_specs=[pl.BlockSpec((tm, w), lambda i: (i, 0)) for w in IN_SPLITS],
        out_shape=[jax.ShapeDtypeStruct((n, w), F32) for w in IN_SPLITS],
        compiler_params=_params("arbitrary"),
        name="inproj",
    )(x, g, w_bf16)


def _mixer_kernel(pos_off, nb, qa_ref, k0_ref, k1_ref, k2_ref, v0_ref, v1_ref, v2_ref, qr_ref, kr_ref, vr_ref,
                  gate_ref, s0_ref, sink_ref, rg_ref, dintra_ref, din_ref, dout_ref, mixed_ref, sfin_ref, s_ref):
    n = pl.program_id(1)

    @pl.when(n == 0)
    def _():
        s_ref[...] = s0_ref[0]

    kk = jnp.concatenate([k0_ref[0], k1_ref[0], k2_ref[0]], axis=0)
    vv = jnp.concatenate([v0_ref[0], v1_ref[0], v2_ref[0]], axis=0)
    qa = qa_ref[0]
    qr = qr_ref[0]
    kr = kr_ref[0] * (HEAD_DIM ** -0.5)
    vr = vr_ref[0]
    gate = gate_ref[0]
    rg = rg_ref[...]
    lk = WINDOW + CHUNK
    rows = ATTN_GROUP * CHUNK
    qi = lax.broadcasted_iota(jnp.int32, (rows, lk), 0)
    kj = lax.broadcasted_iota(jnp.int32, (rows, lk), 1)
    dist = jnp.abs((qi & (CHUNK - 1)) + WINDOW - kj).astype(F32)
    valid = (kj + (pos_off + n * CHUNK)) >= 0
    grp = qi // CHUNK

    def head(a, h):
        return a[:, h * HEAD_DIM:(h + 1) * HEAD_DIM]

    att_s = []
    for kh in range(KV_HEADS):
        q4 = jnp.concatenate([head(qa, kh * ATTN_GROUP + g) for g in range(ATTN_GROUP)], axis=0).astype(BF16)
        att_s.append(lax.dot_general(q4, head(kk, kh).astype(BF16), (((1,), (1,)), ((), ())),
                                     preferred_element_type=F32))
    ret_sc, ret_kv, ret_cross, s_prev = [], [], [], []
    for h in range(RET_HEADS):
        q, k, v = head(qr, h), head(kr, h), head(vr, h).astype(BF16)
        s_prev.append(s_ref[h])
        ret_sc.append(lax.dot_general(q.astype(BF16), k.astype(BF16), (((1,), (1,)), ((), ())),
                                      preferred_element_type=F32))
        ret_kv.append(lax.dot_general((k * dout_ref[h]).astype(BF16), v, (((0,), (0,)), ((), ())),
                                      preferred_element_type=F32))
        ret_cross.append(jnp.dot((q * din_ref[h]).astype(BF16), s_prev[h].astype(BF16), preferred_element_type=F32))

    outs = []
    for kh in range(KV_HEADS):
        slope = jnp.zeros((rows, lk), F32)
        sink = jnp.zeros((rows, 1), F32)
        for g in range(ATTN_GROUP):
            h = kh * ATTN_GROUP + g
            slope = jnp.where(grp == g, ALIBI_SLOPES[h], slope)
            sink = jnp.where(grp[:, 0:1] == g, sink_ref[0, h], sink)
        s = jnp.where(valid, att_s[kh] * (HEAD_DIM ** -0.5) - slope * dist, NEG_INF)
        m = jnp.maximum(jnp.max(s, axis=-1, keepdims=True), sink)
        p = jnp.exp(s - m)
        denom = jnp.sum(p, axis=-1, keepdims=True) + jnp.exp(sink - m)
        o4 = jnp.dot((p / denom).astype(BF16), head(vv, kh).astype(BF16), preferred_element_type=F32)
        outs += [o4[g * CHUNK:(g + 1) * CHUNK] for g in range(ATTN_GROUP)]
    for h in range(RET_HEADS):
        intra = jnp.dot((ret_sc[h] * dintra_ref[h]).astype(BF16), head(vr, h).astype(BF16),
                        preferred_element_type=F32)
        s_ref[h] = math.exp(CHUNK * RET_LOG_DECAY[h]) * s_prev[h] + ret_kv[h]
        o = intra + ret_cross[h]
        mu = jnp.mean(o, axis=-1, keepdims=True)
        oc = o - mu
        var = jnp.mean(oc * oc, axis=-1, keepdims=True)
        gt = head(gate, h)
        outs.append(oc * lax.rsqrt(var + EPS) * head(rg, h) * (gt * jax.nn.sigmoid(gt)))
    mixed_ref[0] = jnp.concatenate(outs, axis=-1)

    @pl.when(n == nb - 1)
    def _():
        sfin_ref[0] = s_ref[...]


def _mixer(qa, kbuf, vbuf, qr, kr, vr, gate, s0, sinks, rg, dintra, din, dout, pos0):
    b, t, _ = qa.shape
    nb = t // CHUNK
    tok = lambda w: pl.BlockSpec((1, CHUNK, w), lambda i, j: (i, j, 0))
    kvs = [pl.BlockSpec((1, CHUNK, KV_WIDTH), functools.partial(lambda i, j, o: (i, j + o, 0), o=o)) for o in range(3)]
    state = pl.BlockSpec((1, RET_HEADS, HEAD_DIM, HEAD_DIM), lambda i, j: (i, 0, 0, 0))
    const3 = pl.BlockSpec((RET_HEADS, CHUNK, HEAD_DIM), lambda i, j: (0, 0, 0))
    return pl.pallas_call(
        functools.partial(_mixer_kernel, pos0 - WINDOW, nb),
        grid=(b, nb),
        in_specs=[tok(ATTN_WIDTH)] + kvs + kvs + [tok(RET_WIDTH)] * 4 + [
            state,
            pl.BlockSpec(memory_space=pltpu.SMEM),
            pl.BlockSpec((1, RET_WIDTH), lambda i, j: (0, 0)),
            const3, const3, const3,
        ],
        out_specs=[pl.BlockSpec((1, CHUNK, D_MODEL), lambda i, j: (i, j, 0)), state],
        out_shape=[
            jax.ShapeDtypeStruct((b, t, D_MODEL), F32),
            jax.ShapeDtypeStruct((b, RET_HEADS, HEAD_DIM, HEAD_DIM), F32),
        ],
        scratch_shapes=[pltpu.VMEM((RET_HEADS, HEAD_DIM, HEAD_DIM), F32)],
        compiler_params=_params("arbitrary", "arbitrary"),
        name="mixer",
    )(qa, kbuf, kbuf, kbuf, vbuf, vbuf, vbuf, qr, kr, vr, gate, s0, sinks, rg, dintra, din, dout)


def _decay_tables():
    lg = np.asarray(RET_LOG_DECAY, np.float32)
    i = np.arange(CHUNK, dtype=np.float32)
    dintra = np.exp(np.abs(i[:, None] - i[None, :])[None] * lg[:, None, None])
    din = np.broadcast_to(np.exp((i + 1.0)[None, :] * lg[:, None])[:, :, None], dintra.shape)
    dout = np.broadcast_to(np.exp((CHUNK - 1.0 - i)[None, :] * lg[:, None])[:, :, None], dintra.shape)
    return tuple(jnp.asarray(a, F32) for a in (dintra, din, dout))


def _topk_rows(s, k, payload=None):
    rows = s.shape[0]
    iota = lax.broadcasted_iota(jnp.int32, s.shape, 0)
    vals, idxs = [], []
    for _ in range(k):
        m = jnp.max(s, axis=0, keepdims=True)
        pos = jnp.min(jnp.where(s == m, iota, rows), axis=0, keepdims=True)
        hit = iota == pos
        vals.append(m)
        if payload is None:
            idxs.append(pos)
        else:
            idxs.append(jnp.max(jnp.where(hit, payload, -1), axis=0, keepdims=True))
        s = jnp.where(hit, -jnp.inf, s)
    return jnp.concatenate(vals, axis=0), jnp.concatenate(idxs, axis=0)


def _pruned_candidates(sv, si):
    k = PEER_TOPK
    row = lax.broadcasted_iota(jnp.int32, (SUBLANES, sv[0].shape[1]), 0)
    sums = [sv[0][0:1, :] + sv[1]]
    idxs = [si[0][0:1, :] * PEER_KEYS + si[1]]
    for a in range(1, SUBLANES):
        s = sv[0][a:a + 1, :] + sv[1][0:SUBLANES, :]
        sums.append(jnp.where(row < k // (a + 1), s, -jnp.inf))
        idxs.append(si[0][a:a + 1, :] * PEER_KEYS + si[1][0:SUBLANES, :])
    sums.append(sv[0][SUBLANES:k, :] + sv[1][0:1, :])
    idxs.append(si[0][SUBLANES:k, :] * PEER_KEYS + si[1][0:1, :])
    return jnp.concatenate(sums, axis=0), jnp.concatenate(idxs, axis=0)


def _route_kernel(mixed_ref, x_ref, wout_ref, fg_ref, wqt_ref, subk_ref, x1_ref, xn_ref, idx_ref, gw_ref):
    x1 = x_ref[...] + jnp.dot(mixed_ref[...].astype(BF16), wout_ref[...], preferred_element_type=F32)
    x1_ref[...] = x1
    xn = _rms(x1, fg_ref[...])
    xn_ref[...] = xn
    qt = lax.dot_general(wqt_ref[...], xn.astype(BF16), (((1,), (1,)), ((), ())), preferred_element_type=F32)
    for h in range(PEER_HEADS):
        sv, si = [], []
        for c in range(2):
            hc = h * 2 + c
            qh = qt[hc * PEER_KEYS:(hc + 1) * PEER_KEYS, :].astype(BF16)
            s = jnp.dot(subk_ref[hc], qh, preferred_element_type=F32)
            v, i = _topk_rows(s, PEER_TOPK)
            sv.append(v)
            si.append(i)
        cand, cidx = _pruned_candidates(sv, si)
        top_s, expert = _topk_rows(cand, PEER_TOPK, payload=cidx)
        e = jnp.exp(top_s - top_s[0:1, :])
        gw_ref[h * PEER_TOPK:(h + 1) * PEER_TOPK, :] = e / jnp.sum(e, axis=0, keepdims=True)
        idx_ref[h * PEER_TOPK:(h + 1) * PEER_TOPK, :] = expert


def _route(mixed, x, wout_bf16, fg, wqt_bf16, subk_bf16):
    n = x.shape[0]
    tm = TOKEN_TILE
    tok = pl.BlockSpec((tm, D_MODEL), lambda i: (i, 0))
    pair = pl.BlockSpec((PEER_PAIRS, tm), lambda i: (0, i))
    return pl.pallas_call(
        _route_kernel,
        grid=(n // tm,),
        in_specs=[
            tok, tok,
            pl.BlockSpec((D_MODEL, D_MODEL), lambda i: (0, 0)),
            pl.BlockSpec((1, D_MODEL), lambda i: (0, 0)),
            pl.BlockSpec((2 * PEER_HEADS * PEER_KEYS, D_MODEL), lambda i: (0, 0)),
            pl.BlockSpec((2 * PEER_HEADS, PEER_KEYS, PEER_KEYS), lambda i: (0, 0, 0)),
        ],
        out_specs=[tok, tok, pair, pair],
        out_shape=[
            jax.ShapeDtypeStruct((n, D_MODEL), F32),
            jax.ShapeDtypeStruct((n, D_MODEL), F32),
            jax.ShapeDtypeStruct((PEER_PAIRS, n), jnp.int32),
            jax.ShapeDtypeStruct((PEER_PAIRS, n), F32),
        ],
        compiler_params=_params("arbitrary"),
        name="route",
    )(mixed, x, wout_bf16, fg, wqt_bf16, subk_bf16)


def _expert_table(t):
    return t.astype(BF16).reshape(t.shape[0], ROW_TILES, LANES)


def _split3(x):
    hi = x.astype(BF16).astype(F32)
    r = x - hi
    mid = r.astype(BF16).astype(F32)
    lo = r - mid
    return jnp.concatenate([hi, mid, lo], axis=0).astype(BF16)


def _sum3(z):
    return (z[0:SUBLANES] + z[SUBLANES:2 * SUBLANES]) + z[2 * SUBLANES:3 * SUBLANES]


def _diag_mask():
    r = lax.broadcasted_iota(jnp.int32, (ROW_TILES, D_MODEL), 0)
    k = lax.broadcasted_iota(jnp.int32, (ROW_TILES, D_MODEL), 1)
    return (k & (ROW_TILES - 1)) == r


def _stage_rows(idx_ref, tab_ref, t, dst_ref):
    experts = idx_ref.at[pl.ds(t * PEER_PAIRS, PEER_PAIRS)]
    for k in range(PEER_PAIRS // 2):
        two = jnp.concatenate([tab_ref[experts[2 * k]], tab_ref[experts[2 * k + 1]]], axis=0)
        dst_ref[2 * ROW_TILES * k:2 * ROW_TILES * (k + 1), :] = two


def _stage_then_consume(idx_ref, tab_ref, stage_refs, consume):
    nb = len(stage_refs)
    _stage_rows(idx_ref, tab_ref, 0, stage_refs[0])

    def body(i, carry):
        t0 = nb * i
        for j in range(nb - 1):
            _stage_rows(idx_ref, tab_ref, t0 + j + 1, stage_refs[j + 1])
            consume(t0 + j, stage_refs[j])
        _stage_rows(idx_ref, tab_ref, jnp.minimum(t0 + nb, PEER_TILE - 1), stage_refs[0])
        consume(t0 + nb - 1, stage_refs[nb - 1])
        return carry

    lax.fori_loop(0, PEER_TILE // nb, body, 0)


def _peer_u_kernel(idx_ref, xn_ref, g_ref, sel_ref, tab_ref, w_ref, *scratch):
    stage_refs, z_ref = scratch[:-1], scratch[-1]
    diag = _diag_mask()

    def consume(t, staged_ref):
        rows = staged_ref[...]
        z = lax.dot_general(_split3(xn_ref[t]), rows, (((1,), (1,)), ((), ())), preferred_element_type=F32)
        z_ref[pl.ds(t, 1), :] = jnp.sum(jnp.where(diag, _sum3(z), 0.0), axis=0, keepdims=True)

    _stage_then_consume(idx_ref, tab_ref, stage_refs, consume)
    zs = z_ref[...]
    hi = zs.astype(BF16)
    r1 = zs - hi.astype(F32)
    mid = r1.astype(BF16)
    lo = (r1 - mid.astype(F32)).astype(BF16)
    sel = sel_ref[...]
    hid = (jnp.dot(hi, sel, preferred_element_type=F32) + jnp.dot(mid, sel, preferred_element_type=F32)
           + jnp.dot(lo, sel, preferred_element_type=F32))
    w_ref[...] = g_ref[...] * (0.5 * hid * (1.0 + lax.erf(hid * math.sqrt(0.5))))


def _peer_specs():
    offs = pl.BlockSpec((PEER_TILE * PEER_PAIRS,), lambda i: (i,), memory_space=pltpu.SMEM)
    row = pl.BlockSpec((PEER_TILE, ROW_TILES, LANES), lambda i: (i, 0, 0))
    table = pl.BlockSpec((PEER_EXPERTS, ROW_TILES, LANES), lambda i: (0, 0, 0), pipeline_mode=pl.Buffered(1))
    staging = [pltpu.VMEM((PEER_PAIRS * ROW_TILES, LANES), BF16)] * STAGE_BUFFERS
    return offs, row, table, staging


def _peer_u(offs, xn3, g, sel, tab):
    n = xn3.shape[0]
    offs_spec, row, table, staging = _peer_specs()
    pair = pl.BlockSpec((PEER_TILE, PEER_PAIRS), lambda i: (i, 0))
    return pl.pallas_call(
        _peer_u_kernel,
        grid=(n // PEER_TILE,),
        in_specs=[offs_spec, row, pair, pl.BlockSpec((D_MODEL, PEER_PAIRS), lambda i: (0, 0)), table],
        out_specs=pair,
        out_shape=jax.ShapeDtypeStruct((n, PEER_PAIRS), F32),
        scratch_shapes=staging + [pltpu.VMEM((PEER_TILE, D_MODEL), F32)],
        compiler_params=_params("arbitrary"),
        name="peer_u",
    )(offs, xn3, g, sel, tab)


def _peer_v_kernel(idx_ref, w8_ref, x1_ref, fg_ref, tab_ref, y_ref, *stage_refs):
    diag = _diag_mask()

    def consume(t, staged_ref):
        rows = staged_ref[...]
        wsel =jnp.where(diag, jnp.broadcast_to(w8_ref[t], (ROW_TILES, D_MODEL)), 0.0)
        o = jnp.dot(_split3(wsel), rows, preferred_element_type=F32)
        y_ref[t] = x1_ref[t] + _sum3(o)

    _stage_then_consume(idx_ref, tab_ref, stage_refs, consume)
    x2 = y_ref[...]
    ms = jnp.sum(jnp.sum(x2 * x2, axis=2, keepdims=True), axis=1, keepdims=True) * (1.0 / D_MODEL)
    y_ref[...] = x2 * lax.rsqrt(ms + EPS) * fg_ref[...]


def _peer_v(offs, w8, x13, fg, tab):
    n = x13.shape[0]
    offs_spec, row, table, staging = _peer_specs()
    return pl.pallas_call(
        _peer_v_kernel,
        grid=(n // PEER_TILE,),
        in_specs=[offs_spec, pl.BlockSpec((PEER_TILE, 1, D_MODEL), lambda i: (i, 0, 0)), row,
                  pl.BlockSpec((ROW_TILES, LANES), lambda i: (0, 0)), table],
        out_specs=row,
        out_shape=jax.ShapeDtypeStruct((n, ROW_TILES, LANES), F32),
        scratch_shapes=staging,
        compiler_params=_params("arbitrary"),
        name="peer_v",
    )(offs, w8, x13, fg, tab)


def _trunk(x3, k_cache, v_cache, s0, pos0, p):
    b, t, _ = x3.shape
    n = b * t
    assert t % CHUNK == 0 and n % TOKEN_TILE == 0 and n % PEER_TILE == 0, (b, t)
    x = x3.reshape(n, D_MODEL)
    qa, ka, va, qr, kr, vr, gate = _inproj(x, p["attn_g"], p["w_in"])
    kbuf = jnp.concatenate([k_cache.reshape(b, WINDOW, KV_WIDTH), ka.reshape(b, t, KV_WIDTH)], axis=1)
    vbuf = jnp.concatenate([v_cache.reshape(b, WINDOW, KV_WIDTH), va.reshape(b, t, KV_WIDTH)], axis=1)
    seq = lambda a: a.reshape(b, t, a.shape[-1])
    mixed, s_new = _mixer(seq(qa), kbuf, vbuf, seq(qr), seq(kr), seq(vr), seq(gate), s0, p["sinks"], p["ret_g"],
                          *p["decay"], pos0)
    x1, xn, idx_t, g_t = _route(mixed.reshape(n, D_MODEL), x, p["w_out"], p["ffn_g"], p["wq_t"], p["subk"])
    offs = idx_t.T.reshape(n * PEER_PAIRS)
    w = _peer_u(offs, xn.reshape(n, ROW_TILES, LANES), g_t.T, p["pair_sum"], p["u_tab"])
    w8 = jnp.repeat(w, ROW_TILES, axis=1).reshape(n, 1, D_MODEL)
    y = _peer_v(offs, w8, x1.reshape(n, ROW_TILES, LANES), p["final_g"], p["v_tab"])
    new_k = kbuf[:, t:].reshape(b, WINDOW, KV_HEADS, HEAD_DIM)
    new_v = vbuf[:, t:].reshape(b, WINDOW, KV_HEADS, HEAD_DIM)
    return y.reshape(b, t, D_MODEL), new_k, new_v, s_new


def kernel(x_prompt, x_sample, cache_attn_k, cache_attn_v, state_ret, attn_norm_g, w_in, attn_sinks, ret_norm_g,
           w_out, ffn_norm_g, peer_w_q, peer_sub_keys, peer_u, peer_v, final_norm_g):
    assert attn_norm_g.shape[0] == 1, "single-layer trunk"
    p = dict(
        attn_g=attn_norm_g[0][None],
        w_in=w_in[0].astype(BF16),
        sinks=attn_sinks[0][None],
        ret_g=ret_norm_g[0][None],
        w_out=w_out[0].astype(BF16),
        ffn_g=ffn_norm_g[0][None],
        wq_t=peer_w_q[0].T.astype(BF16),
        subk=peer_sub_keys[0].reshape(2 * PEER_HEADS, PEER_KEYS, PEER_KEYS).astype(BF16),
        u_tab=_expert_table(peer_u[0]),
        v_tab=_expert_table(peer_v[0]),
        final_g=final_norm_g.reshape(ROW_TILES, LANES),
        decay=_decay_tables(),
        pair_sum=jnp.asarray(np.arange(D_MODEL)[:, None] // ROW_TILES == np.arange(PEER_PAIRS)[None, :], BF16),
    )
    bp = x_prompt.shape[0]
    zero_kv = jnp.zeros((bp, WINDOW, KV_HEADS, HEAD_DIM), F32)
    zero_s = jnp.zeros((bp, RET_HEADS, HEAD_DIM, HEAD_DIM), F32)
    yp, kp, vp, sp = _trunk(x_prompt, zero_kv, zero_kv, zero_s, 0, p)
    ys, ks, vs, ss = _trunk(x_sample, cache_attn_k[0], cache_attn_v[0], state_ret[0], PAST_LEN, p)
    return (yp, ys, kp[None], vp[None], sp[None], ks[None], vs[None], ss[None])
```

```python
import functools
import math

import numpy as np
import jax
import jax.numpy as jnp
from jax import lax
from jax.experimental import pallas as pl
from jax.experimental.pallas import tpu as pltpu

D_MODEL = 1024
CHUNK = 64
WINDOW = 128
HEAD_DIM = 64
ATTN_HEADS = 8
KV_HEADS = 2
ATTN_GROUP = ATTN_HEADS // KV_HEADS
RET_HEADS = 8
ATTN_WIDTH = ATTN_HEADS * HEAD_DIM
KV_WIDTH = KV_HEADS * HEAD_DIM
RET_WIDTH = RET_HEADS * HEAD_DIM
IN_SPLITS = (ATTN_WIDTH, KV_WIDTH, KV_WIDTH, RET_WIDTH, RET_WIDTH, RET_WIDTH, RET_WIDTH)
IN_WIDTH = sum(IN_SPLITS)
PEER_HEADS = 8
PEER_KEYS = 128
PEER_EXPERTS = PEER_KEYS * PEER_KEYS
PEER_TOPK = 16
PEER_PAIRS = PEER_HEADS * PEER_TOPK
EPS = 1e-6
NEG_INF = -1e30
PAST_LEN = 2048

LANES = 128
SUBLANES = 8
ROW_TILES = D_MODEL // LANES
VMEM_LIMIT = 48 * 1024 * 1024

TOKEN_TILE = 256
PEER_TILE = 128
TOKENS_PER_TRIP = 8
CHUNK_PAIRS = 32

BF16 = jnp.bfloat16
F32 = jnp.float32

ALIBI_SLOPES = tuple(float(2.0 ** (-8.0 * h / ATTN_HEADS)) for h in range(1, ATTN_HEADS + 1))
RET_LOG_DECAY = tuple(float(np.log(1.0 - 2.0 ** (-5.0 - h))) for h in range(RET_HEADS))


def _rms(x, g):
    return x * lax.rsqrt(jnp.mean(x * x, axis=-1, keepdims=True) + EPS) * g


def _params(*sem):
    return pltpu.CompilerParams(dimension_semantics=sem, vmem_limit_bytes=VMEM_LIMIT)


def _inproj_kernel(x_ref, g_ref, w_ref, *out_refs):
    h = _rms(x_ref[...], g_ref[...]).astype(BF16)
    proj = jnp.dot(h, w_ref[...], preferred_element_type=F32)
    off = 0
    for o_ref, width in zip(out_refs, IN_SPLITS):
        o_ref[...] = proj[:, off:off + width]
        off += width


def _inproj(x, g, w_bf16):
    n = x.shape[0]
    tm = TOKEN_TILE
    return pl.pallas_call(
        _inproj_kernel,
        grid=(n // tm,),
        in_specs=[
            pl.BlockSpec((tm, D_MODEL), lambda i: (i, 0)),
            pl.BlockSpec((1, D_MODEL), lambda i: (0, 0)),
            pl.BlockSpec((D_MODEL, IN_WIDTH), lambda i: (0, 0)),
        ],
        out_specs=[pl.BlockSpec((tm, w), lambda i: (i, 0)) for w in IN_SPLITS],
        out_shape=[jax.ShapeDtypeStruct((n, w), F32) for w in IN_SPLITS],
        compiler_params=_params("arbitrary"),
        name="inproj",
    )(x, g, w_bf16)


def _mixer_kernel(pos_off, nb, qa_ref, k0_ref, k1_ref, k2_ref, v0_ref, v1_ref, v2_ref, qr_ref, kr_ref, vr_ref,
                  gate_ref, s0_ref, sink_ref, rg_ref, dintra_ref, din_ref, dout_ref, mixed_ref, sfin_ref, s_ref):
    n = pl.program_id(1)

    @pl.when(n == 0)
    def _():
        s_ref[...] = s0_ref[0]

    kk = jnp.concatenate([k0_ref[0], k1_ref[0], k2_ref[0]], axis=0)
    vv = jnp.concatenate([v0_ref[0], v1_ref[0], v2_ref[0]], axis=0)
    qa = qa_ref[0]
    qr = qr_ref[0]
    kr = kr_ref[0] * (HEAD_DIM ** -0.5)
    vr = vr_ref[0]
    gate = gate_ref[0]
    rg = rg_ref[...]
    lk = WINDOW + CHUNK
    rows = ATTN_GROUP * CHUNK
    qi = lax.broadcasted_iota(jnp.int32, (rows, lk), 0)
    kj = lax.broadcasted_iota(jnp.int32, (rows, lk), 1)
    dist = jnp.abs((qi & (CHUNK - 1)) + WINDOW - kj).astype(F32)
    valid = (kj + (pos_off + n * CHUNK)) >= 0
    grp = qi // CHUNK

    def head(a, h):
        return a[:, h * HEAD_DIM:(h + 1) * HEAD_DIM]

    att_s = []
    for kh in range(KV_HEADS):
        q4 = jnp.concatenate([head(qa, kh * ATTN_GROUP + g) for g in range(ATTN_GROUP)], axis=0).astype(BF16)
        att_s.append(lax.dot_general(q4, head(kk, kh).astype(BF16), (((1,), (1,)), ((), ())),
                                     preferred_element_type=F32))
    ret_sc, ret_kv, ret_cross, s_prev = [], [], [], []
    for h in range(RET_HEADS):
        q, k, v = head(qr, h), head(kr, h), head(vr, h).astype(BF16)
        s_prev.append(s_ref[h])
        ret_sc.append(lax.dot_general(q.astype(BF16), k.astype(BF16), (((1,), (1,)), ((), ())),
                                      preferred_element_type=F32))
        ret_kv.append(lax.dot_general((k * dout_ref[h]).astype(BF16), v, (((0,), (0,)), ((), ())),
                                      preferred_element_type=F32))
        ret_cross.append(jnp.dot((q * din_ref[h]).astype(BF16), s_prev[h].astype(BF16), preferred_element_type=F32))

    outs = []
    for kh in range(KV_HEADS):
        slope = jnp.zeros((rows, lk), F32)
        sink = jnp.zeros((rows, 1), F32)
        for g in range(ATTN_GROUP):
            h = kh * ATTN_GROUP + g
            slope = jnp.where(grp == g, ALIBI_SLOPES[h], slope)
            sink = jnp.where(grp[:, 0:1] == g, sink_ref[0, h], sink)
        s = jnp.where(valid, att_s[kh] * (HEAD_DIM ** -0.5) - slope * dist, NEG_INF)
        m = jnp.maximum(jnp.max(s, axis=-1, keepdims=True), sink)
        p = jnp.exp(s - m)
        denom = jnp.sum(p, axis=-1, keepdims=True) + jnp.exp(sink - m)
        o4 = jnp.dot((p / denom).astype(BF16), head(vv, kh).astype(BF16), preferred_element_type=F32)
        outs += [o4[g * CHUNK:(g + 1) * CHUNK] for g in range(ATTN_GROUP)]
    for h in range(RET_HEADS):
        intra = jnp.dot((ret_sc[h] * dintra_ref[h]).astype(BF16), head(vr, h).astype(BF16),
                        preferred_element_type=F32)
        s_ref[h] = math.exp(CHUNK * RET_LOG_DECAY[h]) * s_prev[h] + ret_kv[h]
        o = intra + ret_cross[h]
        mu = jnp.mean(o, axis=-1, keepdims=True)
        oc = o - mu
        var = jnp.mean(oc * oc, axis=-1, keepdims=True)
        gt = head(gate, h)
        outs.append(oc * lax.rsqrt(var + EPS) * head(rg, h) * (gt * jax.nn.sigmoid(gt)))
    mixed_ref[0] = jnp.concatenate(outs, axis=-1)

    @pl.when(n == nb - 1)
    def _():
        sfin_ref[0] = s_ref[...]


def _mixer(qa, kbuf, vbuf, qr, kr, vr, gate, s0, sinks, rg, dintra, din, dout, pos0):
    b, t, _ = qa.shape
    nb = t // CHUNK
    tok = lambda w: pl.BlockSpec((1, CHUNK, w), lambda i, j: (i, j, 0))
    kvs = [pl.BlockSpec((1, CHUNK, KV_WIDTH), functools.partial(lambda i, j, o: (i, j + o, 0), o=o)) for o in range(3)]
    state = pl.BlockSpec((1, RET_HEADS, HEAD_DIM, HEAD_DIM), lambda i, j: (i, 0, 0, 0))
    const3 = pl.BlockSpec((RET_HEADS, CHUNK, HEAD_DIM), lambda i, j: (0, 0, 0))
    return pl.pallas_call(
        functools.partial(_mixer_kernel, pos0 - WINDOW, nb),
        grid=(b, nb),
        in_specs=[tok(ATTN_WIDTH)] + kvs + kvs + [tok(RET_WIDTH)] * 4 + [
            state,
            pl.BlockSpec(memory_space=pltpu.SMEM),
            pl.BlockSpec((1, RET_WIDTH), lambda i, j: (0, 0)),
            const3, const3, const3,
        ],
        out_specs=[pl.BlockSpec((1, CHUNK, D_MODEL), lambda i, j: (i, j, 0)), state],
        out_shape=[
            jax.ShapeDtypeStruct((b, t, D_MODEL), F32),
            jax.ShapeDtypeStruct((b, RET_HEADS, HEAD_DIM, HEAD_DIM), F32),
        ],
        scratch_shapes=[pltpu.VMEM((RET_HEADS, HEAD_DIM, HEAD_DIM), F32)],
        compiler_params=_params("arbitrary", "arbitrary"),
        name="mixer",
    )(qa, kbuf, kbuf, kbuf, vbuf, vbuf, vbuf, qr, kr, vr, gate, s0, sinks, rg, dintra, din, dout)


def _decay_tables():
    lg = np.asarray(RET_LOG_DECAY, np.float32)
    i = np.arange(CHUNK, dtype=np.float32)
    dintra = np.exp(np.abs(i[:, None] - i[None, :])[None] * lg[:, None, None])
    din = np.broadcast_to(np.exp((i + 1.0)[None, :] * lg[:, None])[:, :, None], dintra.shape)
    dout = np.broadcast_to(np.exp((CHUNK - 1.0 - i)[None, :] * lg[:, None])[:, :, None], dintra.shape)
    return tuple(jnp.asarray(a, F32) for a in (dintra, din, dout))


def _topk_rows(s, k, payload=None):
    rows = s.shape[0]
    iota = lax.broadcasted_iota(jnp.int32, s.shape, 0)
    vals, idxs = [], []
    for _ in range(k):
        m = jnp.max(s, axis=0, keepdims=True)
        pos = jnp.min(jnp.where(s == m, iota, rows), axis=0, keepdims=True)
        hit = iota == pos
        vals.append(m)
        if payload is None:
            idxs.append(pos)
        else:
            idxs.append(jnp.max(jnp.where(hit, payload, -1), axis=0, keepdims=True))
        s = jnp.where(hit, -jnp.inf, s)
    return jnp.concatenate(vals, axis=0), jnp.concatenate(idxs, axis=0)


def _pruned_candidates(sv, si):
    k = PEER_TOPK
    row = lax.broadcasted_iota(jnp.int32, (SUBLANES, sv[0].shape[1]), 0)
    sums = [sv[0][0:1, :] + sv[1]]
    idxs = [si[0][0:1, :] * PEER_KEYS + si[1]]
    for a in range(1, SUBLANES):
        s = sv[0][a:a + 1, :] + sv[1][0:SUBLANES, :]
        sums.append(jnp.where(row < k // (a + 1), s, -jnp.inf))
        idxs.append(si[0][a:a + 1, :] * PEER_KEYS + si[1][0:SUBLANES, :])
    sums.append(sv[0][SUBLANES:k, :] + sv[1][0:1, :])
    idxs.append(si[0][SUBLANES:k, :] * PEER_KEYS + si[1][0:1, :])
    return jnp.concatenate(sums, axis=0), jnp.concatenate(idxs, axis=0)


def _route_kernel(mixed_ref, x_ref, wout_ref, fg_ref, wqt_ref, subk_ref, x1_ref, xn_ref, idx_ref, gw_ref):
    x1 = x_ref[...] + jnp.dot(mixed_ref[...].astype(BF16), wout_ref[...], preferred_element_type=F32)
    x1_ref[...] = x1
    xn = _rms(x1, fg_ref[...])
    xn_ref[...] = xn
    qt = lax.dot_general(wqt_ref[...], xn.astype(BF16), (((1,), (1,)), ((), ())), preferred_element_type=F32)
    for h in range(PEER_HEADS):
        sv, si = [], []
        for c in range(2):
            hc = h * 2 + c
            qh = qt[hc * PEER_KEYS:(hc + 1) * PEER_KEYS, :].astype(BF16)
            s = jnp.dot(subk_ref[hc], qh, preferred_element_type=F32)
            v, i = _topk_rows(s, PEER_TOPK)
            sv.append(v)
            si.append(i)
        cand, cidx = _pruned_candidates(sv, si)
        top_s, expert = _topk_rows(cand, PEER_TOPK, payload=cidx)
        e = jnp.exp(top_s - top_s[0:1, :])
        gw_ref[h * PEER_TOPK:(h + 1) * PEER_TOPK, :] = e / jnp.sum(e, axis=0, keepdims=True)
        idx_ref[h * PEER_TOPK:(h + 1) * PEER_TOPK, :] = expert


def _route(mixed, x, wout_bf16, fg, wqt_bf16, subk_bf16):
    n = x.shape[0]
    tm = TOKEN_TILE
    tok = pl.BlockSpec((tm, D_MODEL), lambda i: (i, 0))
    pair = pl.BlockSpec((PEER_PAIRS, tm), lambda i: (0, i))
    return pl.pallas_call(
        _route_kernel,
        grid=(n // tm,),
        in_specs=[
            tok, tok,
            pl.BlockSpec((D_MODEL, D_MODEL), lambda i: (0, 0)),
            pl.BlockSpec((1, D_MODEL), lambda i: (0, 0)),
            pl.BlockSpec((2 * PEER_HEADS * PEER_KEYS, D_MODEL), lambda i: (0, 0)),
            pl.BlockSpec((2 * PEER_HEADS, PEER_KEYS, PEER_KEYS), lambda i: (0, 0, 0)),
        ],
        out_specs=[tok, tok, pair, pair],
        out_shape=[
            jax.ShapeDtypeStruct((n, D_MODEL), F32),
            jax.ShapeDtypeStruct((n, D_MODEL), F32),
            jax.ShapeDtypeStruct((PEER_PAIRS, n), jnp.int32),
            jax.ShapeDtypeStruct((PEER_PAIRS, n), F32),
        ],
        compiler_params=_params("arbitrary"),
        name="route",
    )(mixed, x, wout_bf16, fg, wqt_bf16, subk_bf16)


def _expert_table(t):
    return t.astype(BF16).reshape(t.shape[0], ROW_TILES, LANES)


def _split3(x):
    hi = x.astype(BF16).astype(F32)
    r = x - hi
    mid = r.astype(BF16).astype(F32)
    lo = r - mid
    return jnp.concatenate([hi, mid, lo], axis=0).astype(BF16)


def _sum3(z):
    return (z[0:SUBLANES] + z[SUBLANES:2 * SUBLANES]) + z[2 * SUBLANES:3 * SUBLANES]


def _diag_mask():
    r = lax.broadcasted_iota(jnp.int32, (ROW_TILES, D_MODEL), 0)
    k = lax.broadcasted_iota(jnp.int32, (ROW_TILES, D_MODEL), 1)
    return (k & (ROW_TILES - 1)) == r


def _split3_parts(a):
    hi = a.astype(BF16)
    r = a - hi.astype(F32)
    mid = r.astype(BF16)
    return hi, mid, (r - mid.astype(F32)).astype(BF16)


def _row_to_tile(row):
    return jnp.concatenate([row[:, r * LANES:(r + 1) * LANES] for r in range(ROW_TILES)], axis=0)


def _tile_to_row(tile):
    return jnp.concatenate([tile[r:r + 1, :] for r in range(ROW_TILES)], axis=1)


def _expert_chunk(experts, tab_ref, c):
    return jnp.concatenate([tab_ref[experts[c * CHUNK_PAIRS + j]] for j in range(CHUNK_PAIRS)], axis=0)


def _for_each_token(one_token):
    def body(i, carry):
        for j in range(TOKENS_PER_TRIP):
            one_token(TOKENS_PER_TRIP * i + j)
        return carry

    lax.fori_loop(0, PEER_TILE // TOKENS_PER_TRIP, body, 0)


def _peer_u_kernel(idx_ref, xn_ref, g_ref, sel_ref, tab_ref, w8_ref, z_ref):
    diag = _diag_mask()

    def one_token(t):
        experts = idx_ref.at[pl.ds(t * PEER_PAIRS, PEER_PAIRS)]
        x3 = _split3(_row_to_tile(xn_ref[pl.ds(t, 1), :]))
        zs = []
        for c in range(PEER_PAIRS // CHUNK_PAIRS):
            rows = _expert_chunk(experts, tab_ref, c)
            zs.append(_sum3(lax.dot_general(x3, rows, (((1,), (1,)), ((), ())), preferred_element_type=F32)))
        z = jnp.concatenate(zs, axis=1)
        z_ref[pl.ds(t, 1), :] = jnp.sum(jnp.where(diag, z, 0.0), axis=0, keepdims=True)

    _for_each_token(one_token)
    sel = sel_ref[...]
    hid = sum(jnp.dot(part, sel, preferred_element_type=F32) for part in _split3_parts(z_ref[...]))
    w = g_ref[...] * (0.5 * hid * (1.0 + lax.erf(hid * math.sqrt(0.5))))
    w8_ref[...] = sum(lax.dot_general(part, sel, (((1,), (1,)), ((), ())), preferred_element_type=F32)
                      for part in _split3_parts(w))


def _peer_specs():
    experts = pl.BlockSpec((PEER_TILE * PEER_PAIRS,), lambda i: (i,), memory_space=pltpu.SMEM)
    row = pl.BlockSpec((PEER_TILE, D_MODEL), lambda i: (i, 0))
    table = pl.BlockSpec((PEER_EXPERTS, ROW_TILES, LANES), lambda i: (0, 0, 0), pipeline_mode=pl.Buffered(1))
    return experts, row, table


def _peer_u(experts, xn, g, sel, tab):
    n = xn.shape[0]
    experts_spec, row, table = _peer_specs()
    return pl.pallas_call(
        _peer_u_kernel,
        grid=(n // PEER_TILE,),
        in_specs=[experts_spec, row, pl.BlockSpec((PEER_TILE, PEER_PAIRS), lambda i: (i, 0)),
                  pl.BlockSpec((D_MODEL, PEER_PAIRS), lambda i: (0, 0)), table],
        out_specs=row,
        out_shape=jax.ShapeDtypeStruct((n, D_MODEL), F32),
        scratch_shapes=[pltpu.VMEM((PEER_TILE, D_MODEL), F32)],
        compiler_params=_params("arbitrary"),
        name="peer_u",
    )(experts, xn, g, sel, tab)


def _peer_v_kernel(idx_ref, w8_ref, x1_ref, fg_ref, tab_ref, y_ref):
    diag = _diag_mask()

    def one_token(t):
        experts = idx_ref.at[pl.ds(t * PEER_PAIRS, PEER_PAIRS)]
        w3 = _split3(jnp.where(diag, jnp.broadcast_to(w8_ref[pl.ds(t, 1), :], (ROW_TILES, D_MODEL)), 0.0))
        acc = None
        for c in range(PEER_PAIRS // CHUNK_PAIRS):
            rows = _expert_chunk(experts, tab_ref, c)
            k0 = c * CHUNK_PAIRS * ROW_TILES
            part = jnp.dot(w3[:, k0:k0 + CHUNK_PAIRS * ROW_TILES], rows, preferred_element_type=F32)
            acc = part if acc is None else acc + part
        y_ref[pl.ds(t, 1), :] = x1_ref[pl.ds(t, 1), :] + _tile_to_row(_sum3(acc))

    _for_each_token(one_token)
    y_ref[...] = _rms(y_ref[...], fg_ref[...])


def _peer_v(experts, w8, x1, fg, tab):
    n = x1.shape[0]
    experts_spec, row, table = _peer_specs()
    return pl.pallas_call(
        _peer_v_kernel,
        grid=(n // PEER_TILE,),
        in_specs=[experts_spec, row, row, pl.BlockSpec((1, D_MODEL), lambda i: (0, 0)), table],
        out_specs=row,
        out_shape=jax.ShapeDtypeStruct((n, D_MODEL), F32),
        compiler_params=_params("arbitrary"),
        name="peer_v",
    )(experts, w8, x1, fg, tab)


def _trunk(x3, k_cache, v_cache, s0, pos0, p):
    b, t, _ = x3.shape
    n = b * t
    assert t % CHUNK == 0 and n % TOKEN_TILE == 0 and n % PEER_TILE == 0, (b, t)
    x = x3.reshape(n, D_MODEL)
    qa, ka, va, qr, kr, vr, gate = _inproj(x, p["attn_g"], p["w_in"])
    kbuf = jnp.concatenate([k_cache.reshape(b, WINDOW, KV_WIDTH), ka.reshape(b, t, KV_WIDTH)], axis=1)
    vbuf = jnp.concatenate([v_cache.reshape(b, WINDOW, KV_WIDTH), va.reshape(b, t, KV_WIDTH)], axis=1)
    seq = lambda a: a.reshape(b, t, a.shape[-1])
    mixed, s_new = _mixer(seq(qa), kbuf, vbuf, seq(qr), seq(kr), seq(vr), seq(gate), s0, p["sinks"], p["ret_g"],
                          *p["decay"], pos0)
    x1, xn, idx_t, g_t = _route(mixed.reshape(n, D_MODEL), x, p["w_out"], p["ffn_g"], p["wq_t"], p["subk"])
    experts = idx_t.T.reshape(n * PEER_PAIRS)
    w8 = _peer_u(experts, xn, g_t.T, p["pair_sum"], p["u_tab"])
    y = _peer_v(experts, w8, x1, p["final_g"], p["v_tab"])
    new_k = kbuf[:, t:].reshape(b, WINDOW, KV_HEADS, HEAD_DIM)
    new_v = vbuf[:, t:].reshape(b, WINDOW, KV_HEADS, HEAD_DIM)
    return y.reshape(b, t, D_MODEL), new_k, new_v, s_new


def kernel(x_prompt, x_sample, cache_attn_k, cache_attn_v, state_ret, attn_norm_g, w_in, attn_sinks, ret_norm_g,
           w_out, ffn_norm_g, peer_w_q, peer_sub_keys, peer_u, peer_v, final_norm_g):
    assert attn_norm_g.shape[0] == 1, "single-layer trunk"
    p = dict(
        attn_g=attn_norm_g[0][None],
        w_in=w_in[0].astype(BF16),
        sinks=attn_sinks[0][None],
        ret_g=ret_norm_g[0][None],
        w_out=w_out[0].astype(BF16),
        ffn_g=ffn_norm_g[0][None],
        wq_t=peer_w_q[0].T.astype(BF16),
        subk=peer_sub_keys[0].reshape(2 * PEER_HEADS, PEER_KEYS, PEER_KEYS).astype(BF16),
        u_tab=_expert_table(peer_u[0]),
        v_tab=_expert_table(peer_v[0]),
        final_g=final_norm_g[None],
        decay=_decay_tables(),
        pair_sum=jnp.asarray(np.arange(D_MODEL)[:, None] // ROW_TILES == np.arange(PEER_PAIRS)[None, :], BF16),
    )
    bp = x_prompt.shape[0]
    zero_kv = jnp.zeros((bp, WINDOW, KV_HEADS, HEAD_DIM), F32)
    zero_s = jnp.zeros((bp, RET_HEADS, HEAD_DIM, HEAD_DIM), F32)
    yp, kp, vp, sp = _trunk(x_prompt, zero_kv, zero_kv, zero_s, 0, p)
    ys, ks, vs, ss = _trunk(x_sample, cache_attn_k[0], cache_attn_v[0], state_ret[0], PAST_LEN, p)
    return (yp, ys, kp[None], vp[None], sp[None], ks[None], vs[None], ss[None])
```

```python
import functools
import math

import numpy as np
import jax
import jax.numpy as jnp
from jax import lax
from jax.experimental import pallas as pl
from jax.experimental.pallas import tpu as pltpu

D_MODEL = 1024
CHUNK = 64
WINDOW = 128
HEAD_DIM = 64
ATTN_HEADS = 8
KV_HEADS = 2
ATTN_GROUP = ATTN_HEADS // KV_HEADS
RET_HEADS = 8
ATTN_WIDTH = ATTN_HEADS * HEAD_DIM
KV_WIDTH = KV_HEADS * HEAD_DIM
RET_WIDTH = RET_HEADS * HEAD_DIM
IN_SPLITS = (ATTN_WIDTH, KV_WIDTH, KV_WIDTH, RET_WIDTH, RET_WIDTH, RET_WIDTH, RET_WIDTH)
IN_WIDTH = sum(IN_SPLITS)
PEER_HEADS = 8
PEER_KEYS = 128
PEER_EXPERTS = PEER_KEYS * PEER_KEYS
PEER_TOPK = 16
PEER_PAIRS = PEER_HEADS * PEER_TOPK
EPS = 1e-6
NEG_INF = -1e30
PAST_LEN = 2048

LANES = 128
SUBLANES = 8
ROW_TILES = D_MODEL // LANES
VMEM_LIMIT = 48 * 1024 * 1024

TOKEN_TILE = 256
PEER_TILE = 128
TOKENS_PER_TRIP = 16
CHUNK_PAIRS = 32

BF16 = jnp.bfloat16
F32 = jnp.float32

ALIBI_SLOPES = tuple(float(2.0 ** (-8.0 * h / ATTN_HEADS)) for h in range(1, ATTN_HEADS + 1))
RET_LOG_DECAY = tuple(float(np.log(1.0 - 2.0 ** (-5.0 - h))) for h in range(RET_HEADS))


def _rms(x, g):
    return x * lax.rsqrt(jnp.mean(x * x, axis=-1, keepdims=True) + EPS) * g


def _params(*sem):
    return pltpu.CompilerParams(dimension_semantics=sem, vmem_limit_bytes=VMEM_LIMIT)


def _inproj_kernel(x_ref, g_ref, w_ref, *out_refs):
    h = _rms(x_ref[...], g_ref[...]).astype(BF16)
    proj = jnp.dot(h, w_ref[...], preferred_element_type=F32)
    off = 0
    for o_ref, width in zip(out_refs, IN_SPLITS):
        o_ref[...] = proj[:, off:off + width]
        off += width


def _inproj(x, g, w_bf16):
    n = x.shape[0]
    tm = TOKEN_TILE
    return pl.pallas_call(
        _inproj_kernel,
        grid=(n // tm,),
        in_specs=[
            pl.BlockSpec((tm, D_MODEL), lambda i: (i, 0)),
            pl.BlockSpec((1, D_MODEL), lambda i: (0, 0)),
            pl.BlockSpec((D_MODEL, IN_WIDTH), lambda i: (0, 0)),
        ],
        out_specs=[pl.BlockSpec((tm, w), lambda i: (i, 0)) for w in IN_SPLITS],
        out_shape=[jax.ShapeDtypeStruct((n, w), F32) for w in IN_SPLITS],
        compiler_params=_params("arbitrary"),
        name="inproj",
    )(x, g, w_bf16)


def _mixer_kernel(pos_off, nb, qa_ref, k0_ref, k1_ref, k2_ref, v0_ref, v1_ref, v2_ref, qr_ref, kr_ref, vr_ref,
                  gate_ref, s0_ref, sink_ref, rg_ref, dintra_ref, din_ref, dout_ref, mixed_ref, sfin_ref, s_ref):
    n = pl.program_id(1)

    @pl.when(n == 0)
    def _():
        s_ref[...] = s0_ref[0]

    kk = jnp.concatenate([k0_ref[0], k1_ref[0], k2_ref[0]], axis=0)
    vv = jnp.concatenate([v0_ref[0], v1_ref[0], v2_ref[0]], axis=0)
    qa = qa_ref[0]
    qr = qr_ref[0]
    kr = kr_ref[0] * (HEAD_DIM ** -0.5)
    vr = vr_ref[0]
    gate = gate_ref[0]
    rg = rg_ref[...]
    lk = WINDOW + CHUNK
    rows = ATTN_GROUP * CHUNK
    qi = lax.broadcasted_iota(jnp.int32, (rows, lk), 0)
    kj = lax.broadcasted_iota(jnp.int32, (rows, lk), 1)
    dist = jnp.abs((qi & (CHUNK - 1)) + WINDOW - kj).astype(F32)
    valid = (kj + (pos_off + n * CHUNK)) >= 0
    grp = qi // CHUNK

    def head(a, h):
        return a[:, h * HEAD_DIM:(h + 1) * HEAD_DIM]

    att_s = []
    for kh in range(KV_HEADS):
        q4 = jnp.concatenate([head(qa, kh * ATTN_GROUP + g) for g in range(ATTN_GROUP)], axis=0).astype(BF16)
        att_s.append(lax.dot_general(q4, head(kk, kh).astype(BF16), (((1,), (1,)), ((), ())),
                                     preferred_element_type=F32))
    ret_sc, ret_kv, ret_cross, s_prev = [], [], [], []
    for h in range(RET_HEADS):
        q, k, v = head(qr, h), head(kr, h), head(vr, h).astype(BF16)
        s_prev.append(s_ref[h])
        ret_sc.append(lax.dot_general(q.astype(BF16), k.astype(BF16), (((1,), (1,)), ((), ())),
                                      preferred_element_type=F32))
        ret_kv.append(lax.dot_general((k * dout_ref[h]).astype(BF16), v, (((0,), (0,)), ((), ())),
                                      preferred_element_type=F32))
        ret_cross.append(jnp.dot((q * din_ref[h]).astype(BF16), s_prev[h].astype(BF16), preferred_element_type=F32))

    outs = []
    for kh in range(KV_HEADS):
        slope = jnp.zeros((rows, lk), F32)
        sink = jnp.zeros((rows, 1), F32)
        for g in range(ATTN_GROUP):
            h = kh * ATTN_GROUP + g
            slope = jnp.where(grp == g, ALIBI_SLOPES[h], slope)
            sink = jnp.where(grp[:, 0:1] == g, sink_ref[0, h], sink)
        s = jnp.where(valid, att_s[kh] * (HEAD_DIM ** -0.5) - slope * dist, NEG_INF)
        m = jnp.maximum(jnp.max(s, axis=-1, keepdims=True), sink)
        p = jnp.exp(s - m)
        denom = jnp.sum(p, axis=-1, keepdims=True) + jnp.exp(sink - m)
        o4 = jnp.dot((p / denom).astype(BF16), head(vv, kh).astype(BF16), preferred_element_type=F32)
        outs += [o4[g * CHUNK:(g + 1) * CHUNK] for g in range(ATTN_GROUP)]
    for h in range(RET_HEADS):
        intra = jnp.dot((ret_sc[h] * dintra_ref[h]).astype(BF16), head(vr, h).astype(BF16),
                        preferred_element_type=F32)
        s_ref[h] = math.exp(CHUNK * RET_LOG_DECAY[h]) * s_prev[h] + ret_kv[h]
        o = intra + ret_cross[h]
        mu = jnp.mean(o, axis=-1, keepdims=True)
        oc = o - mu
        var = jnp.mean(oc * oc, axis=-1, keepdims=True)
        gt = head(gate, h)
        outs.append(oc * lax.rsqrt(var + EPS) * head(rg, h) * (gt * jax.nn.sigmoid(gt)))
    mixed_ref[0] = jnp.concatenate(outs, axis=-1)

    @pl.when(n == nb - 1)
    def _():
        sfin_ref[0] = s_ref[...]


def _mixer(qa, kbuf, vbuf, qr, kr, vr, gate, s0, sinks, rg, dintra, din, dout, pos0):
    b, t, _ = qa.shape
    nb = t // CHUNK
    tok = lambda w: pl.BlockSpec((1, CHUNK, w), lambda i, j: (i, j, 0))
    kvs = [pl.BlockSpec((1, CHUNK, KV_WIDTH), functools.partial(lambda i, j, o: (i, j + o, 0), o=o)) for o in range(3)]
    state = pl.BlockSpec((1, RET_HEADS, HEAD_DIM, HEAD_DIM), lambda i, j: (i, 0, 0, 0))
    const3 = pl.BlockSpec((RET_HEADS, CHUNK, HEAD_DIM), lambda i, j: (0, 0, 0))
    return pl.pallas_call(
        functools.partial(_mixer_kernel, pos0 - WINDOW, nb),
        grid=(b, nb),
        in_specs=[tok(ATTN_WIDTH)] + kvs + kvs + [tok(RET_WIDTH)] * 4 + [
            state,
            pl.BlockSpec(memory_space=pltpu.SMEM),
            pl.BlockSpec((1, RET_WIDTH), lambda i, j: (0, 0)),
            const3, const3, const3,
        ],
        out_specs=[pl.BlockSpec((1, CHUNK, D_MODEL), lambda i, j: (i, j, 0)), state],
        out_shape=[
            jax.ShapeDtypeStruct((b, t, D_MODEL), F32),
            jax.ShapeDtypeStruct((b, RET_HEADS, HEAD_DIM, HEAD_DIM), F32),
        ],
        scratch_shapes=[pltpu.VMEM((RET_HEADS, HEAD_DIM, HEAD_DIM), F32)],
        compiler_params=_params("arbitrary", "arbitrary"),
        name="mixer",
    )(qa, kbuf, kbuf, kbuf, vbuf, vbuf, vbuf, qr, kr, vr, gate, s0, sinks, rg, dintra, din, dout)


def _decay_tables():
    lg = np.asarray(RET_LOG_DECAY, np.float32)
    i = np.arange(CHUNK, dtype=np.float32)
    dintra = np.exp(np.abs(i[:, None] - i[None, :])[None] * lg[:, None, None])
    din = np.broadcast_to(np.exp((i + 1.0)[None, :] * lg[:, None])[:, :, None], dintra.shape)
    dout = np.broadcast_to(np.exp((CHUNK - 1.0 - i)[None, :] * lg[:, None])[:, :, None], dintra.shape)
    return tuple(jnp.asarray(a, F32) for a in (dintra, din, dout))


def _topk_rows(s, k, payload=None):
    rows = s.shape[0]
    iota = lax.broadcasted_iota(jnp.int32, s.shape, 0)
    vals, idxs = [], []
    for _ in range(k):
        m = jnp.max(s, axis=0, keepdims=True)
        pos = jnp.min(jnp.where(s == m, iota, rows), axis=0, keepdims=True)
        hit = iota == pos
        vals.append(m)
        if payload is None:
            idxs.append(pos)
        else:
            idxs.append(jnp.max(jnp.where(hit, payload, -1), axis=0, keepdims=True))
        s = jnp.where(hit, -jnp.inf, s)
    return jnp.concatenate(vals, axis=0), jnp.concatenate(idxs, axis=0)


def _pruned_candidates(sv, si):
    k = PEER_TOPK
    row = lax.broadcasted_iota(jnp.int32, (SUBLANES, sv[0].shape[1]), 0)
    sums = [sv[0][0:1, :] + sv[1]]
    idxs = [si[0][0:1, :] * PEER_KEYS + si[1]]
    for a in range(1, SUBLANES):
        s = sv[0][a:a + 1, :] + sv[1][0:SUBLANES, :]
        sums.append(jnp.where(row < k // (a + 1), s, -jnp.inf))
        idxs.append(si[0][a:a + 1, :] * PEER_KEYS + si[1][0:SUBLANES, :])
    sums.append(sv[0][SUBLANES:k, :] + sv[1][0:1, :])
    idxs.append(si[0][SUBLANES:k, :] * PEER_KEYS + si[1][0:1, :])
    return jnp.concatenate(sums, axis=0), jnp.concatenate(idxs, axis=0)


def _route_kernel(mixed_ref, x_ref, wout_ref, fg_ref, wqt_ref, subk_ref, x1_ref, xn_ref, idx_ref, gw_ref):
    x1 = x_ref[...] + jnp.dot(mixed_ref[...].astype(BF16), wout_ref[...], preferred_element_type=F32)
    x1_ref[...] = x1
    xn = _rms(x1, fg_ref[...])
    xn_ref[...] = xn
    qt = lax.dot_general(wqt_ref[...], xn.astype(BF16), (((1,), (1,)), ((), ())), preferred_element_type=F32)
    for h in range(PEER_HEADS):
        sv, si = [], []
        for c in range(2):
            hc = h * 2 + c
            qh = qt[hc * PEER_KEYS:(hc + 1) * PEER_KEYS, :].astype(BF16)
            s = jnp.dot(subk_ref[hc], qh, preferred_element_type=F32)
            v, i = _topk_rows(s, PEER_TOPK)
            sv.append(v)
            si.append(i)
        cand, cidx = _pruned_candidates(sv, si)
        top_s, expert = _topk_rows(cand, PEER_TOPK, payload=cidx)
        e = jnp.exp(top_s - top_s[0:1, :])
        gw_ref[h * PEER_TOPK:(h + 1) * PEER_TOPK, :] = e / jnp.sum(e, axis=0, keepdims=True)
        idx_ref[h * PEER_TOPK:(h + 1) * PEER_TOPK, :] = expert


def _route(mixed, x, wout_bf16, fg, wqt_bf16, subk_bf16):
    n = x.shape[0]
    tm = TOKEN_TILE
    tok = pl.BlockSpec((tm, D_MODEL), lambda i: (i, 0))
    pair = pl.BlockSpec((PEER_PAIRS, tm), lambda i: (0, i))
    return pl.pallas_call(
        _route_kernel,
        grid=(n // tm,),
        in_specs=[
            tok, tok,
            pl.BlockSpec((D_MODEL, D_MODEL), lambda i: (0, 0)),
            pl.BlockSpec((1, D_MODEL), lambda i: (0, 0)),
            pl.BlockSpec((2 * PEER_HEADS * PEER_KEYS, D_MODEL), lambda i: (0, 0)),
            pl.BlockSpec((2 * PEER_HEADS, PEER_KEYS, PEER_KEYS), lambda i: (0, 0, 0)),
        ],
        out_specs=[tok, tok, pair, pair],
        out_shape=[
            jax.ShapeDtypeStruct((n, D_MODEL), F32),
            jax.ShapeDtypeStruct((n, D_MODEL), F32),
            jax.ShapeDtypeStruct((PEER_PAIRS, n), jnp.int32),
            jax.ShapeDtypeStruct((PEER_PAIRS, n), F32),
        ],
        compiler_params=_params("arbitrary"),
        name="route",
    )(mixed, x, wout_bf16, fg, wqt_bf16, subk_bf16)


def _expert_table(t):
    return t.astype(BF16).reshape(t.shape[0], ROW_TILES, LANES)


def _split3(x):
    hi = x.astype(BF16).astype(F32)
    r = x - hi
    mid = r.astype(BF16).astype(F32)
    lo = r - mid
    return jnp.concatenate([hi, mid, lo], axis=0).astype(BF16)


def _sum3(z):
    return (z[0:SUBLANES] + z[SUBLANES:2 * SUBLANES]) + z[2 * SUBLANES:3 * SUBLANES]


def _diag_mask():
    r = lax.broadcasted_iota(jnp.int32, (ROW_TILES, D_MODEL), 0)
    k = lax.broadcasted_iota(jnp.int32, (ROW_TILES, D_MODEL), 1)
    return (k & (ROW_TILES - 1)) == r


def _split3_parts(a):
    hi = a.astype(BF16)
    r = a - hi.astype(F32)
    mid = r.astype(BF16)
    return hi, mid, (r - mid.astype(F32)).astype(BF16)


def _row_to_tile(row):
    return jnp.concatenate([row[:, r * LANES:(r + 1) * LANES] for r in range(ROW_TILES)], axis=0)


def _tile_to_row(tile):
    return jnp.concatenate([tile[r:r + 1, :] for r in range(ROW_TILES)], axis=1)


def _trip_tokens(i):
    return [TOKENS_PER_TRIP * i + j for j in range(TOKENS_PER_TRIP)]


def _trip_experts(idx_ref, tokens):
    return [idx_ref.at[pl.ds(t * PEER_PAIRS, PEER_PAIRS)] for t in tokens]


def _trip_chunk(experts, tab_ref, c):
    slabs = [[None] * CHUNK_PAIRS for _ in experts]
    for q in range(CHUNK_PAIRS):
        for j, token_experts in enumerate(experts):
            slabs[j][q] = tab_ref[token_experts[c * CHUNK_PAIRS + q]]
    return [jnp.concatenate(s, axis=0) for s in slabs]


def _peer_u_kernel(idx_ref, xn_ref, g_ref, sel_ref, tab_ref, w8_ref, z_ref):
    diag = _diag_mask()

    def trip(i, carry):
        tokens = _trip_tokens(i)
        experts = _trip_experts(idx_ref, tokens)
        x3 = [_split3(_row_to_tile(xn_ref[pl.ds(t, 1), :])) for t in tokens]
        zs = [[] for _ in tokens]
        for c in range(PEER_PAIRS // CHUNK_PAIRS):
            for j, rows in enumerate(_trip_chunk(experts, tab_ref, c)):
                zs[j].append(_sum3(lax.dot_general(x3[j], rows, (((1,), (1,)), ((), ())),
                                                   preferred_element_type=F32)))
        for j, t in enumerate(tokens):
            z = jnp.concatenate(zs[j], axis=1)
            z_ref[pl.ds(t, 1), :] = jnp.sum(jnp.where(diag, z, 0.0), axis=0, keepdims=True)
        return carry

    lax.fori_loop(0, PEER_TILE // TOKENS_PER_TRIP, trip, 0)
    sel = sel_ref[...]
    hid = sum(jnp.dot(part, sel, preferred_element_type=F32) for part in _split3_parts(z_ref[...]))
    w = g_ref[...] * (0.5 * hid * (1.0 + lax.erf(hid * math.sqrt(0.5))))
    w8_ref[...] = sum(lax.dot_general(part, sel, (((1,), (1,)), ((), ())), preferred_element_type=F32)
                      for part in _split3_parts(w))


def _peer_specs():
    experts = pl.BlockSpec((PEER_TILE * PEER_PAIRS,), lambda i: (i,), memory_space=pltpu.SMEM)
    row = pl.BlockSpec((PEER_TILE, D_MODEL), lambda i: (i, 0))
    table = pl.BlockSpec((PEER_EXPERTS, ROW_TILES, LANES), lambda i: (0, 0, 0), pipeline_mode=pl.Buffered(1))
    return experts, row, table


def _peer_u(experts, xn, g, sel, tab):
    n = xn.shape[0]
    experts_spec, row, table = _peer_specs()
    return pl.pallas_call(
        _peer_u_kernel,
        grid=(n // PEER_TILE,),
        in_specs=[experts_spec, row, pl.BlockSpec((PEER_TILE, PEER_PAIRS), lambda i: (i, 0)),
                  pl.BlockSpec((D_MODEL, PEER_PAIRS), lambda i: (0, 0)), table],
        out_specs=row,
        out_shape=jax.ShapeDtypeStruct((n, D_MODEL), F32),
        scratch_shapes=[pltpu.VMEM((PEER_TILE, D_MODEL), F32)],
        compiler_params=_params("arbitrary"),
        name="peer_u",
    )(experts, xn, g, sel, tab)


def _peer_v_kernel(idx_ref, w8_ref, x1_ref, fg_ref, tab_ref, y_ref):
    diag = _diag_mask()

    def trip(i, carry):
        tokens = _trip_tokens(i)
        experts = _trip_experts(idx_ref, tokens)
        w3 = [_split3(jnp.where(diag, jnp.broadcast_to(w8_ref[pl.ds(t, 1), :], (ROW_TILES, D_MODEL)), 0.0))
              for t in tokens]
        acc = [None] * len(tokens)
        for c in range(PEER_PAIRS // CHUNK_PAIRS):
            k0 = c * CHUNK_PAIRS * ROW_TILES
            for j, rows in enumerate(_trip_chunk(experts, tab_ref, c)):
                part = jnp.dot(w3[j][:, k0:k0 + CHUNK_PAIRS * ROW_TILES], rows, preferred_element_type=F32)
                acc[j] = part if acc[j] is None else acc[j] + part
        for j, t in enumerate(tokens):
            y_ref[pl.ds(t, 1), :] = x1_ref[pl.ds(t, 1), :] + _tile_to_row(_sum3(acc[j]))
        return carry

    lax.fori_loop(0, PEER_TILE // TOKENS_PER_TRIP, trip, 0)
    y_ref[...] = _rms(y_ref[...], fg_ref[...])


def _peer_v(experts, w8, x1, fg, tab):
    n = x1.shape[0]
    experts_spec, row, table = _peer_specs()
    return pl.pallas_call(
        _peer_v_kernel,
        grid=(n // PEER_TILE,),
        in_specs=[experts_spec, row, row, pl.BlockSpec((1, D_MODEL), lambda i: (0, 0)), table],
        out_specs=row,
        out_shape=jax.ShapeDtypeStruct((n, D_MODEL), F32),
        compiler_params=_params("arbitrary"),
        name="peer_v",
    )(experts, w8, x1, fg, tab)


def _trunk(x3, k_cache, v_cache, s0, pos0, p):
    b, t, _ = x3.shape
    n = b * t
    assert t % CHUNK == 0 and n % TOKEN_TILE == 0 and n % PEER_TILE == 0, (b, t)
    x = x3.reshape(n, D_MODEL)
    qa, ka, va, qr, kr, vr, gate = _inproj(x, p["attn_g"], p["w_in"])
    kbuf = jnp.concatenate([k_cache.reshape(b, WINDOW, KV_WIDTH), ka.reshape(b, t, KV_WIDTH)], axis=1)
    vbuf = jnp.concatenate([v_cache.reshape(b, WINDOW, KV_WIDTH), va.reshape(b, t, KV_WIDTH)], axis=1)
    seq = lambda a: a.reshape(b, t, a.shape[-1])
    mixed, s_new = _mixer(seq(qa), kbuf, vbuf, seq(qr), seq(kr), seq(vr), seq(gate), s0, p["sinks"], p["ret_g"],
                          *p["decay"], pos0)
    x1, xn, idx_t, g_t = _route(mixed.reshape(n, D_MODEL), x, p["w_out"], p["ffn_g"], p["wq_t"], p["subk"])
    experts = idx_t.T.reshape(n * PEER_PAIRS)
    w8 = _peer_u(experts, xn, g_t.T, p["pair_sum"], p["u_tab"])
    y = _peer_v(experts, w8, x1, p["final_g"], p["v_tab"])
    new_k = kbuf[:, t:].reshape(b, WINDOW, KV_HEADS, HEAD_DIM)
    new_v = vbuf[:, t:].reshape(b, WINDOW, KV_HEADS, HEAD_DIM)
    return y.reshape(b, t, D_MODEL), new_k, new_v, s_new


def kernel(x_prompt, x_sample, cache_attn_k, cache_attn_v, state_ret, attn_norm_g, w_in, attn_sinks, ret_norm_g,
           w_out, ffn_norm_g, peer_w_q, peer_sub_keys, peer_u, peer_v, final_norm_g):
    assert attn_norm_g.shape[0] == 1, "single-layer trunk"
    p = dict(
        attn_g=attn_norm_g[0][None],
        w_in=w_in[0].astype(BF16),
        sinks=attn_sinks[0][None],
        ret_g=ret_norm_g[0][None],
        w_out=w_out[0].astype(BF16),
        ffn_g=ffn_norm_g[0][None],
        wq_t=peer_w_q[0].T.astype(BF16),
        subk=peer_sub_keys[0].reshape(2 * PEER_HEADS, PEER_KEYS, PEER_KEYS).astype(BF16),
        u_tab=_expert_table(peer_u[0]),
        v_tab=_expert_table(peer_v[0]),
        final_g=final_norm_g[None],
        decay=_decay_tables(),
        pair_sum=jnp.asarray(np.arange(D_MODEL)[:, None] // ROW_TILES == np.arange(PEER_PAIRS)[None, :], BF16),
    )
    bp = x_prompt.shape[0]
    zero_kv = jnp.zeros((bp, WINDOW, KV_HEADS, HEAD_DIM), F32)
    zero_s = jnp.zeros((bp, RET_HEADS, HEAD_DIM, HEAD_DIM), F32)
    yp, kp, vp, sp = _trunk(x_prompt, zero_kv, zero_kv, zero_s, 0, p)
    ys, ks, vs, ss = _trunk(x_sample, cache_attn_k[0], cache_attn_v[0], state_ret[0], PAST_LEN, p)
    return (yp, ys, kp[None], vp[None], sp[None], ks[None], vs[None], ss[None])
```

```python
import functools
import math

import numpy as np
import jax
import jax.numpy as jnp
from jax import lax
from jax.experimental import pallas as pl
from jax.experimental.pallas import tpu as pltpu

D_MODEL = 1024
CHUNK = 64
WINDOW = 128
HEAD_DIM = 64
ATTN_HEADS = 8
KV_HEADS = 2
ATTN_GROUP = ATTN_HEADS // KV_HEADS
RET_HEADS = 8
ATTN_WIDTH = ATTN_HEADS * HEAD_DIM
KV_WIDTH = KV_HEADS * HEAD_DIM
RET_WIDTH = RET_HEADS * HEAD_DIM
IN_SPLITS = (ATTN_WIDTH, KV_WIDTH, KV_WIDTH, RET_WIDTH, RET_WIDTH, RET_WIDTH, RET_WIDTH)
IN_WIDTH = sum(IN_SPLITS)
PEER_HEADS = 8
PEER_KEYS = 128
PEER_EXPERTS = PEER_KEYS * PEER_KEYS
PEER_TOPK = 16
PEER_PAIRS = PEER_HEADS * PEER_TOPK
EPS = 1e-6
NEG_INF = -1e30
PAST_LEN = 2048

LANES = 128
SUBLANES = 8
ROW_TILES = D_MODEL // LANES
VMEM_LIMIT = 48 * 1024 * 1024

TOKEN_TILE = 256
PEER_TILE = 128
TOKENS_PER_TRIP = 16
U_CHUNK_PAIRS = 32
V_CHUNK_PAIRS = 16

BF16 = jnp.bfloat16
F32 = jnp.float32

ALIBI_SLOPES = tuple(float(2.0 ** (-8.0 * h / ATTN_HEADS)) for h in range(1, ATTN_HEADS + 1))
RET_LOG_DECAY = tuple(float(np.log(1.0 - 2.0 ** (-5.0 - h))) for h in range(RET_HEADS))


def _rms(x, g):
    return x * lax.rsqrt(jnp.mean(x * x, axis=-1, keepdims=True) + EPS) * g


def _params(*sem):
    return pltpu.CompilerParams(dimension_semantics=sem, vmem_limit_bytes=VMEM_LIMIT)


def _inproj_kernel(x_ref, g_ref, w_ref, *out_refs):
    h = _rms(x_ref[...], g_ref[...]).astype(BF16)
    proj = jnp.dot(h, w_ref[...], preferred_element_type=F32)
    off = 0
    for o_ref, width in zip(out_refs, IN_SPLITS):
        o_ref[...] = proj[:, off:off + width]
        off += width


def _inproj(x, g, w_bf16):
    n = x.shape[0]
    tm = TOKEN_TILE
    return pl.pallas_call(
        _inproj_kernel,
        grid=(n // tm,),
        in_specs=[
            pl.BlockSpec((tm, D_MODEL), lambda i: (i, 0)),
            pl.BlockSpec((1, D_MODEL), lambda i: (0, 0)),
            pl.BlockSpec((D_MODEL, IN_WIDTH), lambda i: (0, 0)),
        ],
        out_specs=[pl.BlockSpec((tm, w), lambda i: (i, 0)) for w in IN_SPLITS],
        out_shape=[jax.ShapeDtypeStruct((n, w), F32) for w in IN_SPLITS],
        compiler_params=_params("arbitrary"),
        name="inproj",
    )(x, g, w_bf16)


def _mixer_kernel(pos_off, nb, qa_ref, k0_ref, k1_ref, k2_ref, v0_ref, v1_ref, v2_ref, qr_ref, kr_ref, vr_ref,
                  gate_ref, s0_ref, sink_ref, rg_ref, dintra_ref, din_ref, dout_ref, mixed_ref, sfin_ref, s_ref):
    n = pl.program_id(1)

    @pl.when(n == 0)
    def _():
        s_ref[...] = s0_ref[0]

    kk = jnp.concatenate([k0_ref[0], k1_ref[0], k2_ref[0]], axis=0)
    vv = jnp.concatenate([v0_ref[0], v1_ref[0], v2_ref[0]], axis=0)
    qa = qa_ref[0]
    qr = qr_ref[0]
    kr = kr_ref[0] * (HEAD_DIM ** -0.5)
    vr = vr_ref[0]
    gate = gate_ref[0]
    rg = rg_ref[...]
    lk = WINDOW + CHUNK
    rows = ATTN_GROUP * CHUNK
    qi = lax.broadcasted_iota(jnp.int32, (rows, lk), 0)
    kj = lax.broadcasted_iota(jnp.int32, (rows, lk), 1)
    dist = jnp.abs((qi & (CHUNK - 1)) + WINDOW - kj).astype(F32)
    valid = (kj + (pos_off + n * CHUNK)) >= 0
    grp = qi // CHUNK

    def head(a, h):
        return a[:, h * HEAD_DIM:(h + 1) * HEAD_DIM]

    att_s = []
    for kh in range(KV_HEADS):
        q4 = jnp.concatenate([head(qa, kh * ATTN_GROUP + g) for g in range(ATTN_GROUP)], axis=0).astype(BF16)
        att_s.append(lax.dot_general(q4, head(kk, kh).astype(BF16), (((1,), (1,)), ((), ())),
                                     preferred_element_type=F32))
    ret_sc, ret_kv, ret_cross, s_prev = [], [], [], []
    for h in range(RET_HEADS):
        q, k, v = head(qr, h), head(kr, h), head(vr, h).astype(BF16)
        s_prev.append(s_ref[h])
        ret_sc.append(lax.dot_general(q.astype(BF16), k.astype(BF16), (((1,), (1,)), ((), ())),
                                      preferred_element_type=F32))
        ret_kv.append(lax.dot_general((k * dout_ref[h]).astype(BF16), v, (((0,), (0,)), ((), ())),
                                      preferred_element_type=F32))
        ret_cross.append(jnp.dot((q * din_ref[h]).astype(BF16), s_prev[h].astype(BF16), preferred_element_type=F32))

    outs = []
    for kh in range(KV_HEADS):
        slope = jnp.zeros((rows, lk), F32)
        sink = jnp.zeros((rows, 1), F32)
        for g in range(ATTN_GROUP):
            h = kh * ATTN_GROUP + g
            slope = jnp.where(grp == g, ALIBI_SLOPES[h], slope)
            sink = jnp.where(grp[:, 0:1] == g, sink_ref[0, h], sink)
        s = jnp.where(valid, att_s[kh] * (HEAD_DIM ** -0.5) - slope * dist, NEG_INF)
        m = jnp.maximum(jnp.max(s, axis=-1, keepdims=True), sink)
        p = jnp.exp(s - m)
        denom = jnp.sum(p, axis=-1, keepdims=True) + jnp.exp(sink - m)
        o4 = jnp.dot((p / denom).astype(BF16), head(vv, kh).astype(BF16), preferred_element_type=F32)
        outs += [o4[g * CHUNK:(g + 1) * CHUNK] for g in range(ATTN_GROUP)]
    for h in range(RET_HEADS):
        intra = jnp.dot((ret_sc[h] * dintra_ref[h]).astype(BF16), head(vr, h).astype(BF16),
                        preferred_element_type=F32)
        s_ref[h] = math.exp(CHUNK * RET_LOG_DECAY[h]) * s_prev[h] + ret_kv[h]
        o = intra + ret_cross[h]
        mu = jnp.mean(o, axis=-1, keepdims=True)
        oc = o - mu
        var = jnp.mean(oc * oc, axis=-1, keepdims=True)
        gt = head(gate, h)
        outs.append(oc * lax.rsqrt(var + EPS) * head(rg, h) * (gt * jax.nn.sigmoid(gt)))
    mixed_ref[0] = jnp.concatenate(outs, axis=-1)

    @pl.when(n == nb - 1)
    def _():
        sfin_ref[0] = s_ref[...]


def _mixer(qa, kbuf, vbuf, qr, kr, vr, gate, s0, sinks, rg, dintra, din, dout, pos0):
    b, t, _ = qa.shape
    nb = t // CHUNK
    tok = lambda w: pl.BlockSpec((1, CHUNK, w), lambda i, j: (i, j, 0))
    kvs = [pl.BlockSpec((1, CHUNK, KV_WIDTH), functools.partial(lambda i, j, o: (i, j + o, 0), o=o)) for o in range(3)]
    state = pl.BlockSpec((1, RET_HEADS, HEAD_DIM, HEAD_DIM), lambda i, j: (i, 0, 0, 0))
    const3 = pl.BlockSpec((RET_HEADS, CHUNK, HEAD_DIM), lambda i, j: (0, 0, 0))
    return pl.pallas_call(
        functools.partial(_mixer_kernel, pos0 - WINDOW, nb),
        grid=(b, nb),
        in_specs=[tok(ATTN_WIDTH)] + kvs + kvs + [tok(RET_WIDTH)] * 4 + [
            state,
            pl.BlockSpec(memory_space=pltpu.SMEM),
            pl.BlockSpec((1, RET_WIDTH), lambda i, j: (0, 0)),
            const3, const3, const3,
        ],
        out_specs=[pl.BlockSpec((1, CHUNK, D_MODEL), lambda i, j: (i, j, 0)), state],
        out_shape=[
            jax.ShapeDtypeStruct((b, t, D_MODEL), F32),
            jax.ShapeDtypeStruct((b, RET_HEADS, HEAD_DIM, HEAD_DIM), F32),
        ],
        scratch_shapes=[pltpu.VMEM((RET_HEADS, HEAD_DIM, HEAD_DIM), F32)],
        compiler_params=_params("arbitrary", "arbitrary"),
        name="mixer",
    )(qa, kbuf, kbuf, kbuf, vbuf, vbuf, vbuf, qr, kr, vr, gate, s0, sinks, rg, dintra, din, dout)


def _decay_tables():
    lg = np.asarray(RET_LOG_DECAY, np.float32)
    i = np.arange(CHUNK, dtype=np.float32)
    dintra = np.exp(np.abs(i[:, None] - i[None, :])[None] * lg[:, None, None])
    din = np.broadcast_to(np.exp((i + 1.0)[None, :] * lg[:, None])[:, :, None], dintra.shape)
    dout = np.broadcast_to(np.exp((CHUNK - 1.0 - i)[None, :] * lg[:, None])[:, :, None], dintra.shape)
    return tuple(jnp.asarray(a, F32) for a in (dintra, din, dout))


def _topk_rows(s, k, payload=None):
    rows = s.shape[0]
    iota = lax.broadcasted_iota(jnp.int32, s.shape, 0)
    vals, idxs = [], []
    for _ in range(k):
        m = jnp.max(s, axis=0, keepdims=True)
        pos = jnp.min(jnp.where(s == m, iota, rows), axis=0, keepdims=True)
        hit = iota == pos
        vals.append(m)
        if payload is None:
            idxs.append(pos)
        else:
            idxs.append(jnp.max(jnp.where(hit, payload, -1), axis=0, keepdims=True))
        s = jnp.where(hit, -jnp.inf, s)
    return jnp.concatenate(vals, axis=0), jnp.concatenate(idxs, axis=0)


def _pruned_candidates(sv, si):
    k = PEER_TOPK
    row = lax.broadcasted_iota(jnp.int32, (SUBLANES, sv[0].shape[1]), 0)
    sums = [sv[0][0:1, :] + sv[1]]
    idxs = [si[0][0:1, :] * PEER_KEYS + si[1]]
    for a in range(1, SUBLANES):
        s = sv[0][a:a + 1, :] + sv[1][0:SUBLANES, :]
        sums.append(jnp.where(row < k // (a + 1), s, -jnp.inf))
        idxs.append(si[0][a:a + 1, :] * PEER_KEYS + si[1][0:SUBLANES, :])
    sums.append(sv[0][SUBLANES:k, :] + sv[1][0:1, :])
    idxs.append(si[0][SUBLANES:k, :] * PEER_KEYS + si[1][0:1, :])
    return jnp.concatenate(sums, axis=0), jnp.concatenate(idxs, axis=0)


def _route_kernel(mixed_ref, x_ref, wout_ref, fg_ref, wqt_ref, subk_ref, x1_ref, xn_ref, idx_ref, gw_ref):
    x1 = x_ref[...] + jnp.dot(mixed_ref[...].astype(BF16), wout_ref[...], preferred_element_type=F32)
    x1_ref[...] = x1
    xn = _rms(x1, fg_ref[...])
    xn_ref[...] = xn
    qt = lax.dot_general(wqt_ref[...], xn.astype(BF16), (((1,), (1,)), ((), ())), preferred_element_type=F32)
    for h in range(PEER_HEADS):
        sv, si = [], []
        for c in range(2):
            hc = h * 2 + c
            qh = qt[hc * PEER_KEYS:(hc + 1) * PEER_KEYS, :].astype(BF16)
            s = jnp.dot(subk_ref[hc], qh, preferred_element_type=F32)
            v, i = _topk_rows(s, PEER_TOPK)
            sv.append(v)
            si.append(i)
        cand, cidx = _pruned_candidates(sv, si)
        top_s, expert = _topk_rows(cand, PEER_TOPK, payload=cidx)
        e = jnp.exp(top_s - top_s[0:1, :])
        gw_ref[h * PEER_TOPK:(h + 1) * PEER_TOPK, :] = e / jnp.sum(e, axis=0, keepdims=True)
        idx_ref[h * PEER_TOPK:(h + 1) * PEER_TOPK, :] = expert


def _route(mixed, x, wout_bf16, fg, wqt_bf16, subk_bf16):
    n = x.shape[0]
    tm = TOKEN_TILE
    tok = pl.BlockSpec((tm, D_MODEL), lambda i: (i, 0))
    pair = pl.BlockSpec((PEER_PAIRS, tm), lambda i: (0, i))
    return pl.pallas_call(
        _route_kernel,
        grid=(n // tm,),
        in_specs=[
            tok, tok,
            pl.BlockSpec((D_MODEL, D_MODEL), lambda i: (0, 0)),
            pl.BlockSpec((1, D_MODEL), lambda i: (0, 0)),
            pl.BlockSpec((2 * PEER_HEADS * PEER_KEYS, D_MODEL), lambda i: (0, 0)),
            pl.BlockSpec((2 * PEER_HEADS, PEER_KEYS, PEER_KEYS), lambda i: (0, 0, 0)),
        ],
        out_specs=[tok, tok, pair, pair],
        out_shape=[
            jax.ShapeDtypeStruct((n, D_MODEL), F32),
            jax.ShapeDtypeStruct((n, D_MODEL), F32),
            jax.ShapeDtypeStruct((PEER_PAIRS, n), jnp.int32),
            jax.ShapeDtypeStruct((PEER_PAIRS, n), F32),
        ],
        compiler_params=_params("arbitrary"),
        name="route",
    )(mixed, x, wout_bf16, fg, wqt_bf16, subk_bf16)


def _expert_table(t):
    return t.astype(BF16).reshape(t.shape[0], ROW_TILES, LANES)


def _split3(x):
    hi = x.astype(BF16).astype(F32)
    r = x - hi
    mid = r.astype(BF16).astype(F32)
    lo = r - mid
    return jnp.concatenate([hi, mid, lo], axis=0).astype(BF16)


def _sum3(z):
    return (z[0:SUBLANES] + z[SUBLANES:2 * SUBLANES]) + z[2 * SUBLANES:3 * SUBLANES]


def _diag_mask():
    r = lax.broadcasted_iota(jnp.int32, (ROW_TILES, D_MODEL), 0)
    k = lax.broadcasted_iota(jnp.int32, (ROW_TILES, D_MODEL), 1)
    return (k & (ROW_TILES - 1)) == r


def _split3_parts(a):
    hi = a.astype(BF16)
    r = a - hi.astype(F32)
    mid = r.astype(BF16)
    return hi, mid, (r - mid.astype(F32)).astype(BF16)


def _row_to_tile(row):
    return jnp.concatenate([row[:, r * LANES:(r + 1) * LANES] for r in range(ROW_TILES)], axis=0)


def _tile_to_row(tile):
    return jnp.concatenate([tile[r:r + 1, :] for r in range(ROW_TILES)], axis=1)


def _trip_tokens(i):
    return [TOKENS_PER_TRIP * i + j for j in range(TOKENS_PER_TRIP)]


def _trip_experts(idx_ref, tokens):
    return [idx_ref.at[pl.ds(t * PEER_PAIRS, PEER_PAIRS)] for t in tokens]


def _trip_chunk(experts, tab_ref, c, pairs):
    slabs = [[None] * pairs for _ in experts]
    for q in range(pairs):
        for j, token_experts in enumerate(experts):
            slabs[j][q] = tab_ref[token_experts[c * pairs + q]]
    return [jnp.concatenate(s, axis=0) for s in slabs]


def _peer_u_kernel(idx_ref, xn_ref, g_ref, sel_ref, tab_ref, w8_ref, z_ref):
    diag = _diag_mask()

    def trip(i, carry):
        tokens = _trip_tokens(i)
        experts = _trip_experts(idx_ref, tokens)
        x3 = [_split3(_row_to_tile(xn_ref[pl.ds(t, 1), :])) for t in tokens]
        zs = [[] for _ in tokens]
        for c in range(PEER_PAIRS // U_CHUNK_PAIRS):
            for j, rows in enumerate(_trip_chunk(experts, tab_ref, c, U_CHUNK_PAIRS)):
                zs[j].append(_sum3(lax.dot_general(x3[j], rows, (((1,), (1,)), ((), ())),
                                                   preferred_element_type=F32)))
        for j, t in enumerate(tokens):
            z = jnp.concatenate(zs[j], axis=1)
            z_ref[pl.ds(t, 1), :] = jnp.sum(jnp.where(diag, z, 0.0), axis=0, keepdims=True)
        return carry

    lax.fori_loop(0, PEER_TILE // TOKENS_PER_TRIP, trip, 0)
    sel = sel_ref[...]
    hid = sum(jnp.dot(part, sel, preferred_element_type=F32) for part in _split3_parts(z_ref[...]))
    w = g_ref[...] * (0.5 * hid * (1.0 + lax.erf(hid * math.sqrt(0.5))))
    w8_ref[...] = sum(lax.dot_general(part, sel, (((1,), (1,)), ((), ())), preferred_element_type=F32)
                      for part in _split3_parts(w))


def _peer_specs():
    experts = pl.BlockSpec((PEER_TILE * PEER_PAIRS,), lambda i: (i,), memory_space=pltpu.SMEM)
    row = pl.BlockSpec((PEER_TILE, D_MODEL), lambda i: (i, 0))
    table = pl.BlockSpec((PEER_EXPERTS, ROW_TILES, LANES), lambda i: (0, 0, 0), pipeline_mode=pl.Buffered(1))
    return experts, row, table


def _peer_u(experts, xn, g, sel, tab):
    n = xn.shape[0]
    experts_spec, row, table = _peer_specs()
    return pl.pallas_call(
        _peer_u_kernel,
        grid=(n // PEER_TILE,),
        in_specs=[experts_spec, row, pl.BlockSpec((PEER_TILE, PEER_PAIRS), lambda i: (i, 0)),
                  pl.BlockSpec((D_MODEL, PEER_PAIRS), lambda i: (0, 0)), table],
        out_specs=row,
        out_shape=jax.ShapeDtypeStruct((n, D_MODEL), F32),
        scratch_shapes=[pltpu.VMEM((PEER_TILE, D_MODEL), F32)],
        compiler_params=_params("arbitrary"),
        name="peer_u",
    )(experts, xn, g, sel, tab)


def _peer_v_kernel(idx_ref, w8_ref, x1_ref, fg_ref, tab_ref, y_ref):
    diag = _diag_mask()

    def trip(i, carry):
        tokens = _trip_tokens(i)
        experts = _trip_experts(idx_ref, tokens)
        w3 = [_split3(jnp.where(diag, jnp.broadcast_to(w8_ref[pl.ds(t, 1), :], (ROW_TILES, D_MODEL)), 0.0))
              for t in tokens]
        acc = [None] * len(tokens)
        for c in range(PEER_PAIRS // V_CHUNK_PAIRS):
            k0 = c * V_CHUNK_PAIRS * ROW_TILES
            for j, rows in enumerate(_trip_chunk(experts, tab_ref, c, V_CHUNK_PAIRS)):
                part = jnp.dot(w3[j][:, k0:k0 + V_CHUNK_PAIRS * ROW_TILES], rows, preferred_element_type=F32)
                acc[j] = part if acc[j] is None else acc[j] + part
        for j, t in enumerate(tokens):
            y_ref[pl.ds(t, 1), :] = x1_ref[pl.ds(t, 1), :] + _tile_to_row(_sum3(acc[j]))
        return carry

    lax.fori_loop(0, PEER_TILE // TOKENS_PER_TRIP, trip, 0)
    y_ref[...] = _rms(y_ref[...], fg_ref[...])


def _peer_v(experts, w8, x1, fg, tab):
    n = x1.shape[0]
    experts_spec, row, table = _peer_specs()
    return pl.pallas_call(
        _peer_v_kernel,
        grid=(n // PEER_TILE,),
        in_specs=[experts_spec, row, row, pl.BlockSpec((1, D_MODEL), lambda i: (0, 0)), table],
        out_specs=row,
        out_shape=jax.ShapeDtypeStruct((n, D_MODEL), F32),
        compiler_params=_params("arbitrary"),
        name="peer_v",
    )(experts, w8, x1, fg, tab)


def _trunk(x3, k_cache, v_cache, s0, pos0, p):
    b, t, _ = x3.shape
    n = b * t
    assert t % CHUNK == 0 and n % TOKEN_TILE == 0 and n % PEER_TILE == 0, (b, t)
    x = x3.reshape(n, D_MODEL)
    qa, ka, va, qr, kr, vr, gate = _inproj(x, p["attn_g"], p["w_in"])
    kbuf = jnp.concatenate([k_cache.reshape(b, WINDOW, KV_WIDTH), ka.reshape(b, t, KV_WIDTH)], axis=1)
    vbuf = jnp.concatenate([v_cache.reshape(b, WINDOW, KV_WIDTH), va.reshape(b, t, KV_WIDTH)], axis=1)
    seq = lambda a: a.reshape(b, t, a.shape[-1])
    mixed, s_new = _mixer(seq(qa), kbuf, vbuf, seq(qr), seq(kr), seq(vr), seq(gate), s0, p["sinks"], p["ret_g"],
                          *p["decay"], pos0)
    x1, xn, idx_t, g_t = _route(mixed.reshape(n, D_MODEL), x, p["w_out"], p["ffn_g"], p["wq_t"], p["subk"])
    experts = idx_t.T.reshape(n * PEER_PAIRS)
    w8 = _peer_u(experts, xn, g_t.T, p["pair_sum"], p["u_tab"])
    y = _peer_v(experts, w8, x1, p["final_g"], p["v_tab"])
    new_k = kbuf[:, t:].reshape(b, WINDOW, KV_HEADS, HEAD_DIM)
    new_v = vbuf[:, t:].reshape(b, WINDOW, KV_HEADS, HEAD_DIM)
    return y.reshape(b, t, D_MODEL), new_k, new_v, s_new


def kernel(x_prompt, x_sample, cache_attn_k, cache_attn_v, state_ret, attn_norm_g, w_in, attn_sinks, ret_norm_g,
           w_out, ffn_norm_g, peer_w_q, peer_sub_keys, peer_u, peer_v, final_norm_g):
    assert attn_norm_g.shape[0] == 1, "single-layer trunk"
    p = dict(
        attn_g=attn_norm_g[0][None],
        w_in=w_in[0].astype(BF16),
        sinks=attn_sinks[0][None],
        ret_g=ret_norm_g[0][None],
        w_out=w_out[0].astype(BF16),
        ffn_g=ffn_norm_g[0][None],
        wq_t=peer_w_q[0].T.astype(BF16),
        subk=peer_sub_keys[0].reshape(2 * PEER_HEADS, PEER_KEYS, PEER_KEYS).astype(BF16),
        u_tab=_expert_table(peer_u[0]),
        v_tab=_expert_table(peer_v[0]),
        final_g=final_norm_g[None],
        decay=_decay_tables(),
        pair_sum=jnp.asarray(np.arange(D_MODEL)[:, None] // ROW_TILES == np.arange(PEER_PAIRS)[None, :], BF16),
    )
    bp = x_prompt.shape[0]
    zero_kv = jnp.zeros((bp, WINDOW, KV_HEADS, HEAD_DIM), F32)
    zero_s = jnp.zeros((bp, RET_HEADS, HEAD_DIM, HEAD_DIM), F32)
    yp, kp, vp, sp = _trunk(x_prompt, zero_kv, zero_kv, zero_s, 0, p)
    ys, ks, vs, ss = _trunk(x_sample, cache_attn_k[0], cache_attn_v[0], state_ret[0], PAST_LEN, p)
    return (yp, ys, kp[None], vp[None], sp[None], ks[None], vs[None], ss[None])
```

```python
import functools
import math

import numpy as np
import jax
import jax.numpy as jnp
from jax import lax
from jax.experimental import pallas as pl
from jax.experimental.pallas import tpu as pltpu

D_MODEL = 1024
CHUNK = 64
WINDOW = 128
HEAD_DIM = 64
ATTN_HEADS = 8
KV_HEADS = 2
ATTN_GROUP = ATTN_HEADS // KV_HEADS
RET_HEADS = 8
ATTN_WIDTH = ATTN_HEADS * HEAD_DIM
KV_WIDTH = KV_HEADS * HEAD_DIM
RET_WIDTH = RET_HEADS * HEAD_DIM
IN_SPLITS = (ATTN_WIDTH, KV_WIDTH, KV_WIDTH, RET_WIDTH, RET_WIDTH, RET_WIDTH, RET_WIDTH)
IN_WIDTH = sum(IN_SPLITS)
PEER_HEADS = 8
PEER_KEYS = 128
PEER_EXPERTS = PEER_KEYS * PEER_KEYS
PEER_TOPK = 16
PEER_PAIRS = PEER_HEADS * PEER_TOPK
EPS = 1e-6
NEG_INF = -1e30
PAST_LEN = 2048

LANES = 128
SUBLANES = 8
ROW_TILES = D_MODEL // LANES
VMEM_LIMIT = 48 * 1024 * 1024

TOKEN_TILE = 256
MIXER_CHUNKS_PER_STEP = 2
PEER_TILE = 128
TOKENS_PER_TRIP = 16
U_CHUNK_PAIRS = 32
V_CHUNK_PAIRS = 16

BF16 = jnp.bfloat16
F32 = jnp.float32

ALIBI_SLOPES = tuple(float(2.0 ** (-8.0 * h / ATTN_HEADS)) for h in range(1, ATTN_HEADS + 1))
RET_LOG_DECAY = tuple(float(np.log(1.0 - 2.0 ** (-5.0 - h))) for h in range(RET_HEADS))


def _rms(x, g):
    return x * lax.rsqrt(jnp.mean(x * x, axis=-1, keepdims=True) + EPS) * g


def _params(*sem):
    return pltpu.CompilerParams(dimension_semantics=sem, vmem_limit_bytes=VMEM_LIMIT)


def _inproj_kernel(x_ref, g_ref, w_ref, *out_refs):
    h = _rms(x_ref[...], g_ref[...]).astype(BF16)
    proj = jnp.dot(h, w_ref[...], preferred_element_type=F32)
    off = 0
    for o_ref, width in zip(out_refs, IN_SPLITS):
        o_ref[...] = proj[:, off:off + width]
        off += width


def _inproj(x, g, w_bf16):
    n = x.shape[0]
    tm = TOKEN_TILE
    return pl.pallas_call(
        _inproj_kernel,
        grid=(n // tm,),
        in_specs=[
            pl.BlockSpec((tm, D_MODEL), lambda i: (i, 0)),
            pl.BlockSpec((1, D_MODEL), lambda i: (0, 0)),
            pl.BlockSpec((D_MODEL, IN_WIDTH), lambda i: (0, 0)),
        ],
        out_specs=[pl.BlockSpec((tm, w), lambda i: (i, 0)) for w in IN_SPLITS],
        out_shape=[jax.ShapeDtypeStruct((n, w), F32) for w in IN_SPLITS],
        compiler_params=_params("arbitrary"),
        name="inproj",
    )(x, g, w_bf16)


def _mixer_kernel(pos_off, nsteps, cps, *refs):
    nkv = cps + 2
    qa_ref = refs[0]
    k_refs, v_refs = refs[1:1 + nkv], refs[1 + nkv:1 + 2 * nkv]
    (qr_ref, kr_ref, vr_ref, gate_ref, s0_ref, sink_ref, rg_ref, dintra_ref, din_ref, dout_ref,
     mixed_ref, sfin_ref, s_ref) = refs[1 + 2 * nkv:]
    n = pl.program_id(1)

    @pl.when(n == 0)
    def _():
        s_ref[...] = s0_ref[0]

    kk_all = jnp.concatenate([r[0] for r in k_refs], axis=0)
    vv_all = jnp.concatenate([r[0] for r in v_refs], axis=0)
    rg = rg_ref[...]
    lk = WINDOW + CHUNK
    rows = ATTN_GROUP * CHUNK
    qi = lax.broadcasted_iota(jnp.int32, (rows, lk), 0)
    kj = lax.broadcasted_iota(jnp.int32, (rows, lk), 1)
    dist = jnp.abs((qi & (CHUNK - 1)) + WINDOW - kj).astype(F32)
    grp = qi // CHUNK

    def head(a, h):
        return a[:, h * HEAD_DIM:(h + 1) * HEAD_DIM]

    state = [s_ref[h] for h in range(RET_HEADS)]
    for c in range(cps):
        tok = slice(c * CHUNK, (c + 1) * CHUNK)
        kk = kk_all[c * CHUNK:c * CHUNK + lk]
        vv = vv_all[c * CHUNK:c * CHUNK + lk]
        qa = qa_ref[0, tok, :]
        qr = qr_ref[0, tok, :]
        kr = kr_ref[0, tok, :] * (HEAD_DIM ** -0.5)
        vr = vr_ref[0, tok, :]
        gate = gate_ref[0, tok, :]
        valid = (kj + (pos_off + (n * cps + c) * CHUNK)) >= 0
        att_s = []
        for kh in range(KV_HEADS):
            q4 = jnp.concatenate([head(qa, kh * ATTN_GROUP + g) for g in range(ATTN_GROUP)], axis=0).astype(BF16)
            att_s.append(lax.dot_general(q4, head(kk, kh).astype(BF16), (((1,), (1,)), ((), ())),
                                         preferred_element_type=F32))
        ret_sc, ret_kv, ret_cross = [], [], []
        for h in range(RET_HEADS):
            q, k, v = head(qr, h), head(kr, h), head(vr, h).astype(BF16)
            ret_sc.append(lax.dot_general(q.astype(BF16), k.astype(BF16), (((1,), (1,)), ((), ())),
                                          preferred_element_type=F32))
            ret_kv.append(lax.dot_general((k * dout_ref[h]).astype(BF16), v, (((0,), (0,)), ((), ())),
                                          preferred_element_type=F32))
            ret_cross.append(jnp.dot((q * din_ref[h]).astype(BF16), state[h].astype(BF16),
                                     preferred_element_type=F32))
        outs = []
        for kh in range(KV_HEADS):
            slope = jnp.zeros((rows, lk), F32)
            sink = jnp.zeros((rows, 1), F32)
            for g in range(ATTN_GROUP):
                h = kh * ATTN_GROUP + g
                slope = jnp.where(grp == g, ALIBI_SLOPES[h], slope)
                sink = jnp.where(grp[:, 0:1] == g, sink_ref[0, h], sink)
            s = jnp.where(valid, att_s[kh] * (HEAD_DIM ** -0.5) - slope * dist, NEG_INF)
            m = jnp.maximum(jnp.max(s, axis=-1, keepdims=True), sink)
            p = jnp.exp(s - m)
            denom = jnp.sum(p, axis=-1, keepdims=True) + jnp.exp(sink - m)
            o4 = jnp.dot((p / denom).astype(BF16), head(vv, kh).astype(BF16), preferred_element_type=F32)
            outs += [o4[g * CHUNK:(g + 1) * CHUNK] for g in range(ATTN_GROUP)]
        for h in range(RET_HEADS):
            intra = jnp.dot((ret_sc[h] * dintra_ref[h]).astype(BF16), head(vr, h).astype(BF16),
                            preferred_element_type=F32)
            o = intra + ret_cross[h]
            state[h] = math.exp(CHUNK * RET_LOG_DECAY[h]) * state[h] + ret_kv[h]
            mu = jnp.mean(o, axis=-1, keepdims=True)
            oc = o - mu
            var = jnp.mean(oc * oc, axis=-1, keepdims=True)
            gt = head(gate, h)
            outs.append(oc * lax.rsqrt(var + EPS) * head(rg, h) * (gt * jax.nn.sigmoid(gt)))
        mixed_ref[0, tok, :] = jnp.concatenate(outs, axis=-1)
    for h in range(RET_HEADS):
        s_ref[h] = state[h]

    @pl.when(n == nsteps - 1)
    def _():
        sfin_ref[0] = s_ref[...]


def _mixer(qa, kbuf, vbuf, qr, kr, vr, gate, s0, sinks, rg, dintra, din, dout, pos0):
    b, t, _ = qa.shape
    nb = t // CHUNK
    cps = MIXER_CHUNKS_PER_STEP if nb % MIXER_CHUNKS_PER_STEP == 0 else 1
    nsteps = nb // cps
    tok = lambda w: pl.BlockSpec((1, cps * CHUNK, w), lambda i, j: (i, j, 0))
    kvs = [pl.BlockSpec((1, CHUNK, KV_WIDTH), functools.partial(lambda i, j, o: (i, j * cps + o, 0), o=o))
           for o in range(cps + 2)]
    state = pl.BlockSpec((1, RET_HEADS, HEAD_DIM, HEAD_DIM), lambda i, j: (i, 0, 0, 0))
    const3 = pl.BlockSpec((RET_HEADS, CHUNK, HEAD_DIM), lambda i, j: (0, 0, 0))
    return pl.pallas_call(
        functools.partial(_mixer_kernel, pos0 - WINDOW, nsteps, cps),
        grid=(b, nsteps),
        in_specs=[tok(ATTN_WIDTH)] + kvs + kvs + [tok(RET_WIDTH)] * 4 + [
            state,
            pl.BlockSpec(memory_space=pltpu.SMEM),
            pl.BlockSpec((1, RET_WIDTH), lambda i, j: (0, 0)),
            const3, const3, const3,
        ],
        out_specs=[pl.BlockSpec((1, cps * CHUNK, D_MODEL), lambda i, j: (i, j, 0)), state],
        out_shape=[
            jax.ShapeDtypeStruct((b, t, D_MODEL), F32),
            jax.ShapeDtypeStruct((b, RET_HEADS, HEAD_DIM, HEAD_DIM), F32),
        ],
        scratch_shapes=[pltpu.VMEM((RET_HEADS, HEAD_DIM, HEAD_DIM), F32)],
        compiler_params=_params("arbitrary", "arbitrary"),
        name="mixer",
    )(qa, *([kbuf] * (cps + 2)), *([vbuf] * (cps + 2)), qr, kr, vr, gate, s0, sinks, rg, dintra, din, dout)


def _decay_tables():
    lg = np.asarray(RET_LOG_DECAY, np.float32)
    i = np.arange(CHUNK, dtype=np.float32)
    dintra = np.exp(np.abs(i[:, None] - i[None, :])[None] * lg[:, None, None])
    din = np.broadcast_to(np.exp((i + 1.0)[None, :] * lg[:, None])[:, :, None], dintra.shape)
    dout = np.broadcast_to(np.exp((CHUNK - 1.0 - i)[None, :] * lg[:, None])[:, :, None], dintra.shape)
    return tuple(jnp.asarray(a, F32) for a in (dintra, din, dout))


def _topk_rows(s, k, payload=None):
    rows = s.shape[0]
    iota = lax.broadcasted_iota(jnp.int32, s.shape, 0)
    vals, idxs = [], []
    for _ in range(k):
        m = jnp.max(s, axis=0, keepdims=True)
        pos = jnp.min(jnp.where(s == m, iota, rows), axis=0, keepdims=True)
        hit = iota == pos
        vals.append(m)
        if payload is None:
            idxs.append(pos)
        else:
            idxs.append(jnp.max(jnp.where(hit, payload, -1), axis=0, keepdims=True))
        s = jnp.where(hit, -jnp.inf, s)
    return jnp.concatenate(vals, axis=0), jnp.concatenate(idxs, axis=0)


def _pruned_candidates(sv, si):
    k = PEER_TOPK
    row = lax.broadcasted_iota(jnp.int32, (SUBLANES, sv[0].shape[1]), 0)
    sums = [sv[0][0:1, :] + sv[1]]
    idxs = [si[0][0:1, :] * PEER_KEYS + si[1]]
    for a in range(1, SUBLANES):
        s = sv[0][a:a + 1, :] + sv[1][0:SUBLANES, :]
        sums.append(jnp.where(row < k // (a + 1), s, -jnp.inf))
        idxs.append(si[0][a:a + 1, :] * PEER_KEYS + si[1][0:SUBLANES, :])
    sums.append(sv[0][SUBLANES:k, :] + sv[1][0:1, :])
    idxs.append(si[0][SUBLANES:k, :] * PEER_KEYS + si[1][0:1, :])
    return jnp.concatenate(sums, axis=0), jnp.concatenate(idxs, axis=0)


def _route_kernel(mixed_ref, x_ref, wout_ref, fg_ref, wqt_ref, subk_ref, x1_ref, xn_ref, idx_ref, gw_ref):
    x1 = x_ref[...] + jnp.dot(mixed_ref[...].astype(BF16), wout_ref[...], preferred_element_type=F32)
    x1_ref[...] = x1
    xn = _rms(x1, fg_ref[...])
    xn_ref[...] = xn
    qt = lax.dot_general(wqt_ref[...], xn.astype(BF16), (((1,), (1,)), ((), ())), preferred_element_type=F32)
    for h in range(PEER_HEADS):
        sv, si = [], []
        for c in range(2):
            hc = h * 2 + c
            qh = qt[hc * PEER_KEYS:(hc + 1) * PEER_KEYS, :].astype(BF16)
            s = jnp.dot(subk_ref[hc], qh, preferred_element_type=F32)
            v, i = _topk_rows(s, PEER_TOPK)
            sv.append(v)
            si.append(i)
        cand, cidx = _pruned_candidates(sv, si)
        top_s, expert = _topk_rows(cand, PEER_TOPK, payload=cidx)
        e = jnp.exp(top_s - top_s[0:1, :])
        gw_ref[h * PEER_TOPK:(h + 1) * PEER_TOPK, :] = e / jnp.sum(e, axis=0, keepdims=True)
        idx_ref[h * PEER_TOPK:(h + 1) * PEER_TOPK, :] = expert


def _route(mixed, x, wout_bf16, fg, wqt_bf16, subk_bf16):
    n = x.shape[0]
    tm = TOKEN_TILE
    tok = pl.BlockSpec((tm, D_MODEL), lambda i: (i, 0))
    pair = pl.BlockSpec((PEER_PAIRS, tm), lambda i: (0, i))
    return pl.pallas_call(
        _route_kernel,
        grid=(n // tm,),
        in_specs=[
            tok, tok,
            pl.BlockSpec((D_MODEL, D_MODEL), lambda i: (0, 0)),
            pl.BlockSpec((1, D_MODEL), lambda i: (0, 0)),
            pl.BlockSpec((2 * PEER_HEADS * PEER_KEYS, D_MODEL), lambda i: (0, 0)),
            pl.BlockSpec((2 * PEER_HEADS, PEER_KEYS, PEER_KEYS), lambda i: (0, 0, 0)),
        ],
        out_specs=[tok, tok, pair, pair],
        out_shape=[
            jax.ShapeDtypeStruct((n, D_MODEL), F32),
            jax.ShapeDtypeStruct((n, D_MODEL), F32),
            jax.ShapeDtypeStruct((PEER_PAIRS, n), jnp.int32),
            jax.ShapeDtypeStruct((PEER_PAIRS, n), F32),
        ],
        compiler_params=_params("arbitrary"),
        name="route",
    )(mixed, x, wout_bf16, fg, wqt_bf16, subk_bf16)


def _expert_table(t):
    return t.astype(BF16).reshape(t.shape[0], ROW_TILES, LANES)


def _split3(x):
    hi = x.astype(BF16).astype(F32)
    r = x - hi
    mid = r.astype(BF16).astype(F32)
    lo = r - mid
    return jnp.concatenate([hi, mid, lo], axis=0).astype(BF16)


def _sum3(z):
    return (z[0:SUBLANES] + z[SUBLANES:2 * SUBLANES]) + z[2 * SUBLANES:3 * SUBLANES]


def _diag_mask():
    r = lax.broadcasted_iota(jnp.int32, (ROW_TILES, D_MODEL), 0)
    k = lax.broadcasted_iota(jnp.int32, (ROW_TILES, D_MODEL), 1)
    return (k & (ROW_TILES - 1)) == r


def _split3_parts(a):
    hi = a.astype(BF16)
    r = a - hi.astype(F32)
    mid = r.astype(BF16)
    return hi, mid, (r - mid.astype(F32)).astype(BF16)


def _row_to_tile(row):
    return jnp.concatenate([row[:, r * LANES:(r + 1) * LANES] for r in range(ROW_TILES)], axis=0)


def _tile_to_row(tile):
    return jnp.concatenate([tile[r:r + 1, :] for r in range(ROW_TILES)], axis=1)


def _trip_tokens(i):
    return [TOKENS_PER_TRIP * i + j for j in range(TOKENS_PER_TRIP)]


def _trip_experts(idx_ref, tokens):
    return [idx_ref.at[pl.ds(t * PEER_PAIRS, PEER_PAIRS)] for t in tokens]


def _trip_chunk(experts, tab_ref, c, pairs):
    slabs = [[None] * pairs for _ in experts]
    for q in range(pairs):
        for j, token_experts in enumerate(experts):
            slabs[j][q] = tab_ref[token_experts[c * pairs + q]]
    return [jnp.concatenate(s, axis=0) for s in slabs]


def _peer_u_kernel(idx_ref, xn_ref, g_ref, sel_ref, tab_ref, w8_ref, z_ref):
    diag = _diag_mask()

    def trip(i, carry):
        tokens = _trip_tokens(i)
        experts = _trip_experts(idx_ref, tokens)
        x3 = [_split3(_row_to_tile(xn_ref[pl.ds(t, 1), :])) for t in tokens]
        zs = [[] for _ in tokens]
        for c in range(PEER_PAIRS // U_CHUNK_PAIRS):
            for j, rows in enumerate(_trip_chunk(experts, tab_ref, c, U_CHUNK_PAIRS)):
                zs[j].append(_sum3(lax.dot_general(x3[j], rows, (((1,), (1,)), ((), ())),
                                                   preferred_element_type=F32)))
        for j, t in enumerate(tokens):
            z = jnp.concatenate(zs[j], axis=1)
            z_ref[pl.ds(t, 1), :] = jnp.sum(jnp.where(diag, z, 0.0), axis=0, keepdims=True)
        return carry

    lax.fori_loop(0, PEER_TILE // TOKENS_PER_TRIP, trip, 0)
    sel = sel_ref[...]
    hid = sum(jnp.dot(part, sel, preferred_element_type=F32) for part in _split3_parts(z_ref[...]))
    w = g_ref[...] * (0.5 * hid * (1.0 + lax.erf(hid * math.sqrt(0.5))))
    w8_ref[...] = sum(lax.dot_general(part, sel, (((1,), (1,)), ((), ())), preferred_element_type=F32)
                      for part in _split3_parts(w))


def _peer_specs():
    experts = pl.BlockSpec((PEER_TILE * PEER_PAIRS,), lambda i: (i,), memory_space=pltpu.SMEM)
    row = pl.BlockSpec((PEER_TILE, D_MODEL), lambda i: (i, 0))
    table = pl.BlockSpec((PEER_EXPERTS, ROW_TILES, LANES), lambda i: (0, 0, 0), pipeline_mode=pl.Buffered(1))
    return experts, row, table


def _peer_u(experts, xn, g, sel, tab):
    n = xn.shape[0]
    experts_spec, row, table = _peer_specs()
    return pl.pallas_call(
        _peer_u_kernel,
        grid=(n // PEER_TILE,),
        in_specs=[experts_spec, row, pl.BlockSpec((PEER_TILE, PEER_PAIRS), lambda i: (i, 0)),
                  pl.BlockSpec((D_MODEL, PEER_PAIRS), lambda i: (0, 0)), table],
        out_specs=row,
        out_shape=jax.ShapeDtypeStruct((n, D_MODEL), F32),
        scratch_shapes=[pltpu.VMEM((PEER_TILE, D_MODEL), F32)],
        compiler_params=_params("arbitrary"),
        name="peer_u",
    )(experts, xn, g, sel, tab)


def _peer_v_kernel(idx_ref, w8_ref, x1_ref, fg_ref, tab_ref, y_ref):
    diag = _diag_mask()

    def trip(i, carry):
        tokens = _trip_tokens(i)
        experts = _trip_experts(idx_ref, tokens)
        w3 = [_split3(jnp.where(diag, jnp.broadcast_to(w8_ref[pl.ds(t, 1), :], (ROW_TILES, D_MODEL)), 0.0))
              for t in tokens]
        acc = [None] * len(tokens)
        for c in range(PEER_PAIRS // V_CHUNK_PAIRS):
            k0 = c * V_CHUNK_PAIRS * ROW_TILES
            for j, rows in enumerate(_trip_chunk(experts, tab_ref, c, V_CHUNK_PAIRS)):
                part = jnp.dot(w3[j][:, k0:k0 + V_CHUNK_PAIRS * ROW_TILES], rows, preferred_element_type=F32)
                acc[j] = part if acc[j] is None else acc[j] + part
        for j, t in enumerate(tokens):
            y_ref[pl.ds(t, 1), :] = x1_ref[pl.ds(t, 1), :] + _tile_to_row(_sum3(acc[j]))
        return carry

    lax.fori_loop(0, PEER_TILE // TOKENS_PER_TRIP, trip, 0)
    y_ref[...] = _rms(y_ref[...], fg_ref[...])


def _peer_v(experts, w8, x1, fg, tab):
    n = x1.shape[0]
    experts_spec, row, table = _peer_specs()
    return pl.pallas_call(
        _peer_v_kernel,
        grid=(n // PEER_TILE,),
        in_specs=[experts_spec, row, row, pl.BlockSpec((1, D_MODEL), lambda i: (0, 0)), table],
        out_specs=row,
        out_shape=jax.ShapeDtypeStruct((n, D_MODEL), F32),
        compiler_params=_params("arbitrary"),
        name="peer_v",
    )(experts, w8, x1, fg, tab)


def _trunk(x3, k_cache, v_cache, s0, pos0, p):
    b, t, _ = x3.shape
    n = b * t
    assert t % CHUNK == 0 and n % TOKEN_TILE == 0 and n % PEER_TILE == 0, (b, t)
    x = x3.reshape(n, D_MODEL)
    qa, ka, va, qr, kr, vr, gate = _inproj(x, p["attn_g"], p["w_in"])
    kbuf = jnp.concatenate([k_cache.reshape(b, WINDOW, KV_WIDTH), ka.reshape(b, t, KV_WIDTH)], axis=1)
    vbuf = jnp.concatenate([v_cache.reshape(b, WINDOW, KV_WIDTH), va.reshape(b, t, KV_WIDTH)], axis=1)
    seq = lambda a: a.reshape(b, t, a.shape[-1])
    mixed, s_new = _mixer(seq(qa), kbuf, vbuf, seq(qr), seq(kr), seq(vr), seq(gate), s0, p["sinks"], p["ret_g"],
                          *p["decay"], pos0)
    x1, xn, idx_t, g_t = _route(mixed.reshape(n, D_MODEL), x, p["w_out"], p["ffn_g"], p["wq_t"], p["subk"])
    experts = idx_t.T.reshape(n * PEER_PAIRS)
    w8 = _peer_u(experts, xn, g_t.T, p["pair_sum"], p["u_tab"])
    y = _peer_v(experts, w8, x1, p["final_g"], p["v_tab"])
    new_k = kbuf[:, t:].reshape(b, WINDOW, KV_HEADS, HEAD_DIM)
    new_v = vbuf[:, t:].reshape(b, WINDOW, KV_HEADS, HEAD_DIM)
    return y.reshape(b, t, D_MODEL), new_k, new_v, s_new


def kernel(x_prompt, x_sample, cache_attn_k, cache_attn_v, state_ret, attn_norm_g, w_in, attn_sinks, ret_norm_g,
           w_out, ffn_norm_g, peer_w_q, peer_sub_keys, peer_u, peer_v, final_norm_g):
    assert attn_norm_g.shape[0] == 1, "single-layer trunk"
    p = dict(
        attn_g=attn_norm_g[0][None],
        w_in=w_in[0].astype(BF16),
        sinks=attn_sinks[0][None],
        ret_g=ret_norm_g[0][None],
        w_out=w_out[0].astype(BF16),
        ffn_g=ffn_norm_g[0][None],
        wq_t=peer_w_q[0].T.astype(BF16),
        subk=peer_sub_keys[0].reshape(2 * PEER_HEADS, PEER_KEYS, PEER_KEYS).astype(BF16),
        u_tab=_expert_table(peer_u[0]),
        v_tab=_expert_table(peer_v[0]),
        final_g=final_norm_g[None],
        decay=_decay_tables(),
        pair_sum=jnp.asarray(np.arange(D_MODEL)[:, None] // ROW_TILES == np.arange(PEER_PAIRS)[None, :], BF16),
    )
    bp = x_prompt.shape[0]
    zero_kv = jnp.zeros((bp, WINDOW, KV_HEADS, HEAD_DIM), F32)
    zero_s = jnp.zeros((bp, RET_HEADS, HEAD_DIM, HEAD_DIM), F32)
    yp, kp, vp, sp = _trunk(x_prompt, zero_kv, zero_kv, zero_s, 0, p)
    ys, ks, vs, ss = _trunk(x_sample, cache_attn_k[0], cache_attn_v[0], state_ret[0], PAST_LEN, p)
    return (yp, ys, kp[None], vp[None], sp[None], ks[None], vs[None], ss[None])
```
